```python
import math
import jax, jax.numpy as jnp
from jax import lax
import numpy as np

D_MODEL = 1024
BATCH = 4
SEQ = 8192
DEPTH = 1

PLE_DIM = 256
MLA_HEADS = 8
QK_NOPE = 64
QK_ROPE = 32
V_HEAD = 64
Q_LORA = 384
KV_LORA = 256
ROPE_THETA = 10000.0
Q_BLOCK = 128
SSM_WIDTH = 512
SSM_GROUP = 16
SSM_GROUPS = SSM_WIDTH // SSM_GROUP
SSM_STATE = 64
DT_MIN = 1e-3
DT_MAX = 1e-1
D_FF = 2816
LN_EPS = 1e-5
RMS_EPS = 1e-6
ALPHA = (2.0 * DEPTH) ** 0.25
BETA = (8.0 * DEPTH) ** -0.25
MLA_WIDTH = MLA_HEADS * V_HEAD
SPLIT_POINTS = (Q_LORA,
                Q_LORA + KV_LORA,
                Q_LORA + KV_LORA + QK_ROPE,
                Q_LORA + KV_LORA + QK_ROPE + SSM_WIDTH,
                Q_LORA + KV_LORA + QK_ROPE + SSM_WIDTH + D_MODEL)
IN_COLS = Q_LORA + KV_LORA + QK_ROPE + SSM_WIDTH + 2 * D_MODEL

kernel_name = "hybrid_mla_s5_macaron_deepnorm_encoder"


def layer_norm(x, g, b):
    xf = x.astype(jnp.float32)
    mu = jnp.mean(xf, axis=-1, keepdims=True)
    var = jnp.mean(jnp.square(xf - mu), axis=-1, keepdims=True)
    y = (xf - mu) * lax.rsqrt(var + LN_EPS)
    return (y * g.astype(jnp.float32) + b.astype(jnp.float32)).astype(x.dtype)


def rms_norm(x, g):
    xf = x.astype(jnp.float32)
    y = xf * lax.rsqrt(jnp.mean(jnp.square(xf), axis=-1, keepdims=True) + RMS_EPS)
    return (y * g.astype(jnp.float32)).astype(x.dtype)


def swiglu(x, w1, w3, w2):
    return (jax.nn.silu(x @ w1) * (x @ w3)) @ w2


def rope_tables(seq_len, dtype):
    pos = jnp.arange(seq_len, dtype=jnp.float32)
    inv_freq = ROPE_THETA ** (-jnp.arange(0, QK_ROPE, 2, dtype=jnp.float32) / QK_ROPE)
    ang = pos[:, None] * inv_freq[None, :]
    return jnp.cos(ang).astype(dtype), jnp.sin(ang).astype(dtype)


def apply_rope(x, cos, sin):
    half = x.shape[-1] // 2
    x1, x2 = x[..., :half], x[..., half:]
    return jnp.concatenate([x1 * cos - x2 * sin, x2 * cos + x1 * sin], axis=-1)


def mla_attention(q_lat, kv_lat, k_rope, q_norm_g, kv_norm_g, w_uq, w_uk, w_uv):
    B, S, _ = q_lat.shape
    cos, sin = rope_tables(S, q_lat.dtype)
    c_q = rms_norm(q_lat, q_norm_g)
    q = jnp.einsum('bsr,rhd->bshd', c_q, w_uq)
    q_nope = q[..., :QK_NOPE]
    q_rope = apply_rope(q[..., QK_NOPE:], cos[:, None, :], sin[:, None, :])
    c_kv = rms_norm(kv_lat, kv_norm_g)
    k_nope = jnp.einsum('bsr,rhd->bshd', c_kv, w_uk)
    v = jnp.einsum('bsr,rhd->bshd', c_kv, w_uv)
    k_rope = apply_rope(k_rope, cos, sin)
    scale = (QK_NOPE + QK_ROPE) ** -0.5
    n_blk = S // Q_BLOCK
    qn_blk = q_nope.reshape(B, n_blk, Q_BLOCK, MLA_HEADS, QK_NOPE).transpose(1, 0, 2, 3, 4)
    qr_blk = q_rope.reshape(B, n_blk, Q_BLOCK, MLA_HEADS, QK_ROPE).transpose(1, 0, 2, 3, 4)

    def attend(blk):
        qn, qr = blk
        s = (jnp.einsum('bqhd,bkhd->bhqk', qn, k_nope)
             + jnp.einsum('bqhd,bkd->bhqk', qr, k_rope))
        pr = jax.nn.softmax(s.astype(jnp.float32) * scale, axis=-1).astype(v.dtype)
        return jnp.einsum('bhqk,bkhd->bqhd', pr, v)

    o = lax.map(attend, (qn_blk, qr_blk))
    return o.transpose(1, 0, 2, 3, 4).reshape(B, S, MLA_WIDTH)


def _ssm_combine(left, right):
    a_l, b_l = left
    a_r, b_r = right
    return a_r * a_l, a_r * b_l + b_r


def s5_direction(u, lam_re, lam_im, log_dt, b_re, b_im, c_re, c_im, reverse):
    S = u.shape[1]
    f32 = jnp.float32
    lam = lax.complex(lam_re.astype(f32), lam_im.astype(f32))
    dt = jnp.exp(log_dt.astype(f32))[:, None]
    lam_bar = jnp.exp(lam * dt)
    b = lax.complex(b_re.astype(f32), b_im.astype(f32))
    b_bar = ((lam_bar - 1.0) / lam)[..., None] * b
    bu = jnp.einsum('bsgc,gpc->sbgp', u.astype(jnp.complex64), b_bar)
    a = jnp.broadcast_to(lam_bar[None, None], (S, 1) + lam_bar.shape)
    _, xs = lax.associative_scan(_ssm_combine, (a, bu), reverse=reverse, axis=0)
    c = lax.complex(c_re.astype(f32), c_im.astype(f32))
    return jnp.einsum('sbgp,gcp->bsgc', xs, c).real


def s5_branch(u, f_params, b_params, d_skip, w_glu):
    B, S, _ = u.shape
    ug = u.astype(jnp.float32).reshape(B, S, SSM_GROUPS, SSM_GROUP)
    y = (s5_direction(ug, *f_params, reverse=False)
         + s5_direction(ug, *b_params, reverse=True)
         + d_skip.astype(jnp.float32) * ug)
    y = y.reshape(B, S, SSM_WIDTH).astype(u.dtype)
    z = jax.nn.gelu(y)
    return z * jax.nn.sigmoid(z @ w_glu)


def setup_inputs(seed: int = 0) -> dict:
    key = jax.random.key(seed)
    ks = iter(jax.random.split(key, 64))
    f32 = jnp.float32
    L, D, G, P, C = DEPTH, D_MODEL, SSM_GROUPS, SSM_STATE, SSM_GROUP

    def nrm(shape, scale):
        return jax.random.normal(next(ks), shape, f32) * scale

    def gain(n):
        return 1.0 + nrm((L, n), 0.02)

    def bias(n):
        return nrm((L, n), 0.02)

    def ssm_dir():
        lam_re = -0.5 + nrm((L, G, P), 0.01)
        lam_im = math.pi * jnp.arange(P, dtype=f32) + nrm((L, G, P), 0.01)
        log_dt = jax.random.uniform(next(ks), (L, G), f32, math.log(DT_MIN), math.log(DT_MAX))
        b_re = nrm((L, G, P, C), (2.0 * C) ** -0.5)
        b_im = nrm((L, G, P, C), (2.0 * C) ** -0.5)
        c_re = nrm((L, G, C, P), (2.0 * P) ** -0.5)
        c_im = nrm((L, G, C, P), (2.0 * P) ** -0.5)
        return lam_re, lam_im, log_dt, b_re, b_im, c_re, c_im

    x = nrm((BATCH, SEQ, D), 1.0)
    p = nrm((DEPTH, BATCH, SEQ, PLE_DIM), 1.0)
    ffn1_w1 = nrm((L, D, D_FF), D ** -0.5)
    ffn1_w3 = nrm((L, D, D_FF), D ** -0.5)
    ffn1_w2 = nrm((L, D_FF, D), BETA * D_FF ** -0.5)
    ln1_g, ln1_b = gain(D), bias(D)
    w_in = nrm((L, D, IN_COLS), D ** -0.5)
    q_norm_g = gain(Q_LORA)
    kv_norm_g = gain(KV_LORA)
    w_uq = nrm((L, Q_LORA, MLA_HEADS, QK_NOPE + QK_ROPE), Q_LORA ** -0.5)
    w_uk = nrm((L, KV_LORA, MLA_HEADS, QK_NOPE), KV_LORA ** -0.5)
    w_uv = nrm((L, KV_LORA, MLA_HEADS, V_HEAD), BETA * KV_LORA ** -0.5)
    w_o_attn = nrm((L, MLA_WIDTH, D), BETA * MLA_WIDTH ** -0.5)
    (ssm_lam_re_f, ssm_lam_im_f, ssm_log_dt_f, ssm_b_re_f, ssm_b_im_f,
     ssm_c_re_f, ssm_c_im_f) = ssm_dir()
    (ssm_lam_re_b, ssm_lam_im_b, ssm_log_dt_b, ssm_b_re_b, ssm_b_im_b,
     ssm_c_re_b, ssm_c_im_b) = ssm_dir()
    ssm_d = nrm((L, G, C), 1.0)
    w_glu = nrm((L, SSM_WIDTH, SSM_WIDTH), SSM_WIDTH ** -0.5)
    w_o_ssm = nrm((L, SSM_WIDTH, D), BETA * SSM_WIDTH ** -0.5)
    w_out = nrm((L, D, D), BETA * D ** -0.5)
    ln2_g, ln2_b = gain(D), bias(D)
    ffn2_w1 = nrm((L, D, D_FF), D ** -0.5)
    ffn2_w3 = nrm((L, D, D_FF), D ** -0.5)
    ffn2_w2 = nrm((L, D_FF, D), BETA * D_FF ** -0.5)
    ln3_g, ln3_b = gain(D), bias(D)
    ple_w_proj = nrm((L, PLE_DIM, D), BETA * PLE_DIM ** -0.5)
    ple_w_gate = nrm((L, D, D), D ** -0.5)
    ln4_g, ln4_b = gain(D), bias(D)
    return {
        "x": x, "p": p,
        "ffn1_w1": ffn1_w1, "ffn1_w3": ffn1_w3, "ffn1_w2": ffn1_w2,
        "ln1_g": ln1_g, "ln1_b": ln1_b,
        "w_in": w_in, "q_norm_g": q_norm_g, "kv_norm_g": kv_norm_g,
        "w_uq": w_uq, "w_uk": w_uk, "w_uv": w_uv, "w_o_attn": w_o_attn,
        "ssm_lam_re_f": ssm_lam_re_f, "ssm_lam_im_f": ssm_lam_im_f, "ssm_log_dt_f": ssm_log_dt_f,
        "ssm_b_re_f": ssm_b_re_f, "ssm_b_im_f": ssm_b_im_f,
        "ssm_c_re_f": ssm_c_re_f, "ssm_c_im_f": ssm_c_im_f,
        "ssm_lam_re_b": ssm_lam_re_b, "ssm_lam_im_b": ssm_lam_im_b, "ssm_log_dt_b": ssm_log_dt_b,
        "ssm_b_re_b": ssm_b_re_b, "ssm_b_im_b": ssm_b_im_b,
        "ssm_c_re_b": ssm_c_re_b, "ssm_c_im_b": ssm_c_im_b,
        "ssm_d": ssm_d, "w_glu": w_glu, "w_o_ssm": w_o_ssm, "w_out": w_out,
        "ln2_g": ln2_g, "ln2_b": ln2_b,
        "ffn2_w1": ffn2_w1, "ffn2_w3": ffn2_w3, "ffn2_w2": ffn2_w2,
        "ln3_g": ln3_g, "ln3_b": ln3_b,
        "ple_w_proj": ple_w_proj, "ple_w_gate": ple_w_gate,
        "ln4_g": ln4_g, "ln4_b": ln4_b,
    }


def reference(x, p, ffn1_w1, ffn1_w3, ffn1_w2, ln1_g, ln1_b,
              w_in, q_norm_g, kv_norm_g, w_uq, w_uk, w_uv, w_o_attn,
              ssm_lam_re_f, ssm_lam_im_f, ssm_log_dt_f, ssm_b_re_f, ssm_b_im_f,
              ssm_c_re_f, ssm_c_im_f,
              ssm_lam_re_b, ssm_lam_im_b, ssm_log_dt_b, ssm_b_re_b, ssm_b_im_b,
              ssm_c_re_b, ssm_c_im_b,
              ssm_d, w_glu, w_o_ssm, w_out, ln2_g, ln2_b,
              ffn2_w1, ffn2_w3, ffn2_w2, ln3_g, ln3_b,
              ple_w_proj, ple_w_gate, ln4_g, ln4_b):
    for i in range(DEPTH):
        x = layer_norm(ALPHA * x + 0.5 * swiglu(x, ffn1_w1[i], ffn1_w3[i], ffn1_w2[i]),
                       ln1_g[i], ln1_b[i])
        proj = x @ w_in[i]
        q_lat, kv_lat, k_rope, u, g_a, g_b = jnp.split(proj, SPLIT_POINTS, axis=-1)
        y_a = mla_attention(q_lat, kv_lat, k_rope, q_norm_g[i], kv_norm_g[i],
                            w_uq[i], w_uk[i], w_uv[i]) @ w_o_attn[i]
        f_params = (ssm_lam_re_f[i], ssm_lam_im_f[i], ssm_log_dt_f[i],
                    ssm_b_re_f[i], ssm_b_im_f[i], ssm_c_re_f[i], ssm_c_im_f[i])
        b_params = (ssm_lam_re_b[i], ssm_lam_im_b[i], ssm_log_dt_b[i],
                    ssm_b_re_b[i], ssm_b_im_b[i], ssm_c_re_b[i], ssm_c_im_b[i])
        y_b = s5_branch(u, f_params, b_params, ssm_d[i].reshape(SSM_GROUPS, SSM_GROUP),
                        w_glu[i]) @ w_o_ssm[i]
        merged = jax.nn.sigmoid(g_a) * y_a + jax.nn.sigmoid(g_b) * y_b
        x = layer_norm(ALPHA * x + merged @ w_out[i], ln2_g[i], ln2_b[i])
        x = layer_norm(ALPHA * x + 0.5 * swiglu(x, ffn2_w1[i], ffn2_w3[i], ffn2_w2[i]),
                       ln3_g[i], ln3_b[i])
        ple = jax.nn.sigmoid(x @ ple_w_gate[i]) * (p[i] @ ple_w_proj[i])
        x = layer_norm(ALPHA * x + ple, ln4_g[i], ln4_b[i])
    return x
```

```python
import functools
import math

import jax
import jax.numpy as jnp
from jax import lax
from jax.experimental import pallas as pl
from jax.experimental.pallas import tpu as pltpu

F32 = jnp.float32
BF16 = jnp.bfloat16

DEPTH = 1
MLA_HEADS = 8
QK_NOPE = 64
QK_ROPE = 32
V_HEAD = 64
Q_LORA = 384
KV_LORA = 256
ROPE_THETA = 10000.0
SSM_GROUP = 16
SSM_STATE = 64
LN_EPS = 1e-5
RMS_EPS = 1e-6
ALPHA = (2.0 * DEPTH) ** 0.25

LANES = 128
SUBLANES = 8
VMEM_LIMIT_BYTES = 56 * 1024 * 1024

HEAD_PAD = LANES
S5_CHUNK = 16
S5_GROUP_BLOCK = 4


def _const_spec(shape):
    nd = len(shape)
    return pl.BlockSpec(shape, lambda *_: (0,) * nd, pipeline_mode=pl.Buffered(1))


def _layer_norm(r, g, b):
    mu = jnp.mean(r, axis=-1, keepdims=True)
    c = r - mu
    var = jnp.mean(c * c, axis=-1, keepdims=True)
    return c * lax.rsqrt(var + LN_EPS) * g + b


def _rms_norm(x, g):
    return x * lax.rsqrt(jnp.mean(x * x, axis=-1, keepdims=True) + RMS_EPS) * g


def _swiglu(xb, w1_ref, w3_ref, w2_ref):
    h1 = jnp.dot(xb, w1_ref[...], preferred_element_type=F32)
    h3 = jnp.dot(xb, w3_ref[...], preferred_element_type=F32)
    a = (h1 * jax.nn.sigmoid(h1) * h3).astype(BF16)
    return jnp.dot(a, w2_ref[...], preferred_element_type=F32)


def _ffn_ln_kernel(x_ref, w1_ref, w3_ref, w2_ref, g_ref, b_ref, o_ref):
    x = x_ref[...]
    y = _swiglu(x.astype(BF16), w1_ref, w3_ref, w2_ref)
    o_ref[...] = _layer_norm(ALPHA * x + 0.5 * y, g_ref[...], b_ref[...])


def _ffn_ln(x, w1, w3, w2, g, b, *, tm):
    n, d = x.shape
    f = w1.shape[1]
    return pl.pallas_call(
        _ffn_ln_kernel,
        grid=(n // tm,),
        in_specs=[pl.BlockSpec((tm, d), lambda i: (i, 0)),
                  _const_spec((d, f)), _const_spec((d, f)), _const_spec((f, d)),
                  _const_spec((1, d)), _const_spec((1, d))],
        out_specs=pl.BlockSpec((tm, d), lambda i: (i, 0)),
        out_shape=jax.ShapeDtypeStruct((n, d), F32),
        compiler_params=pltpu.CompilerParams(
            dimension_semantics=("arbitrary",), vmem_limit_bytes=VMEM_LIMIT_BYTES),
        name="ffn_ln",
    )(x, w1, w3, w2, g, b)


def _ffn_ple_ln_kernel(x_ref, p_ref, w1_ref, w3_ref, w2_ref, g3_ref, b3_ref,
                       wpg_ref, wpp_ref, g4_ref, b4_ref, o_ref):
    x = x_ref[...]
    y = _swiglu(x.astype(BF16), w1_ref, w3_ref, w2_ref)
    x3 = _layer_norm(ALPHA * x + 0.5 * y, g3_ref[...], b3_ref[...])
    gate = jax.nn.sigmoid(jnp.dot(x3.astype(BF16), wpg_ref[...], preferred_element_type=F32))
    emb = jnp.dot(p_ref[...].astype(BF16), wpp_ref[...], preferred_element_type=F32)
    o_ref[...] = _layer_norm(ALPHA * x3 + gate * emb, g4_ref[...], b4_ref[...])


def _ffn_ple_ln(x, p, w1, w3, w2, g3, b3, wpg, wpp, g4, b4, *, tm):
    n, d = x.shape
    f = w1.shape[1]
    pd = p.shape[1]
    return pl.pallas_call(
        _ffn_ple_ln_kernel,
        grid=(n // tm,),
        in_specs=[pl.BlockSpec((tm, d), lambda i: (i, 0)),
                  pl.BlockSpec((tm, pd), lambda i: (i, 0)),
                  _const_spec((d, f)), _const_spec((d, f)), _const_spec((f, d)),
                  _const_spec((1, d)), _const_spec((1, d)),
                  _const_spec((d, d)), _const_spec((pd, d)),
                  _const_spec((1, d)), _const_spec((1, d))],
        out_specs=pl.BlockSpec((tm, d), lambda i: (i, 0)),
        out_shape=jax.ShapeDtypeStruct((n, d), F32),
        compiler_params=pltpu.CompilerParams(
            dimension_semantics=("arbitrary",), vmem_limit_bytes=VMEM_LIMIT_BYTES),
        name="ffn_ple_ln",
    )(x, p, w1, w3, w2, g3, b3, wpg, wpp, g4, b4)


def _proj_mla_kernel(x_ref, win_ref, gq_ref, gkv_ref, wq_ref, wqr_ref, wk_ref, wv_ref,
                     e_ref, cosq_ref, sinq_ref, tk_ref,
                     q_ref, k_ref, v_ref, u_ref, sa_ref, sb_ref, *, d, ssm_w):
    xb = x_ref[...].astype(BF16)
    proj = jnp.dot(xb, win_ref[...], preferred_element_type=F32)
    o_kv = Q_LORA
    o_u = o_kv + KV_LORA
    o_ga = o_u + ssm_w
    o_gb = o_ga + d
    o_kr = o_gb + d
    c_q = _rms_norm(proj[:, :Q_LORA], gq_ref[...]).astype(BF16)
    c_kv = _rms_norm(proj[:, o_kv:o_u], gkv_ref[...]).astype(BF16)
    u_ref[...] = proj[:, o_u:o_ga].astype(BF16)
    sa_ref[...] = jax.nn.sigmoid(proj[:, o_ga:o_gb]).astype(BF16)
    sb_ref[...] = jax.nn.sigmoid(proj[:, o_gb:o_kr]).astype(BF16)
    kr = (proj[:, o_kr:o_kr + LANES] * tk_ref[...]).astype(BF16)
    q = jnp.dot(c_q, wq_ref[...], preferred_element_type=F32)
    q_rot = jnp.dot(c_q, wqr_ref[...], preferred_element_type=F32)
    k = (jnp.dot(c_kv, wk_ref[...], preferred_element_type=F32)
         + jnp.dot(kr, e_ref[...], preferred_element_type=F32))
    v = jnp.dot(c_kv, wv_ref[...], preferred_element_type=F32)
    cosq = cosq_ref[...]
    sinq = sinq_ref[...]
    for h in range(MLA_HEADS):
        sl = slice(h * HEAD_PAD, (h + 1) * HEAD_PAD)
        q_ref[0, h] = (q[:, sl] * cosq + q_rot[:, sl] * sinq).astype(BF16)
        k_ref[0, h] = k[:, sl].astype(BF16)
        v_ref[0, h] = v[:, sl].astype(BF16)


def _proj_mla(x1, win, gq, gkv, wq, wqr, wk, wv, e2, cosq, sinq, tkr, *, batch, seq, tm, ssm_w):
    n, d = x1.shape
    wcols = win.shape[1]
    hp = MLA_HEADS * HEAD_PAD
    spb = seq // tm
    head_spec = pl.BlockSpec((1, MLA_HEADS, tm, HEAD_PAD), lambda i: (i // spb, 0, i % spb, 0))
    head_shape = jax.ShapeDtypeStruct((batch, MLA_HEADS, seq, HEAD_PAD), BF16)
    tab_spec = pl.BlockSpec((tm, LANES), lambda i: (i % spb, 0))
    return pl.pallas_call(
        functools.partial(_proj_mla_kernel, d=d, ssm_w=ssm_w),
        grid=(n // tm,),
        in_specs=[pl.BlockSpec((tm, d), lambda i: (i, 0)),
                  _const_spec((d, wcols)),
                  _const_spec((1, Q_LORA)), _const_spec((1, KV_LORA)),
                  _const_spec((Q_LORA, hp)), _const_spec((Q_LORA, hp)),
                  _const_spec((KV_LORA, hp)), _const_spec((KV_LORA, hp)),
                  _const_spec((LANES, hp)),
                  tab_spec, tab_spec, tab_spec],
        out_specs=[head_spec, head_spec, head_spec,
                   pl.BlockSpec((tm, ssm_w), lambda i: (i, 0)),
                   pl.BlockSpec((tm, d), lambda i: (i, 0)),
                   pl.BlockSpec((tm, d), lambda i: (i, 0))],
        out_shape=[head_shape, head_shape, head_shape,
                   jax.ShapeDtypeStruct((n, ssm_w), BF16),
                   jax.ShapeDtypeStruct((n, d), BF16),
                   jax.ShapeDtypeStruct((n, d), BF16)],
        compiler_params=pltpu.CompilerParams(
            dimension_semantics=("arbitrary",), vmem_limit_bytes=VMEM_LIMIT_BYTES),
        name="proj_mla",
    )(x1, win, gq, gkv, wq, wqr, wk, wv, e2, cosq, sinq, tkr)


def _flash_kernel(q_ref, k_ref, v_ref, o_ref, m_sc, l_sc, acc_sc, *, tk, nk):
    q = q_ref[0, 0]
    m_sc[...] = jnp.full(m_sc.shape, -jnp.inf, F32)
    l_sc[...] = jnp.zeros(l_sc.shape, F32)
    acc_sc[...] = jnp.zeros(acc_sc.shape, F32)

    def body(j, carry):
        off = pl.multiple_of(j * tk, tk)
        k = k_ref[0, 0, pl.ds(off, tk), :]
        v = v_ref[0, 0, pl.ds(off, tk), :]
        s = lax.dot_general(q, k, (((1,), (1,)), ((), ())), preferred_element_type=F32)
        m_prev = m_sc[...]
        m_new = jnp.maximum(m_prev, jnp.max(s, axis=1, keepdims=True))
        alpha = jnp.exp(m_prev - m_new)
        p = jnp.exp(s - m_new)
        l_sc[...] = alpha * l_sc[...] + jnp.sum(p, axis=1, keepdims=True)
        acc_sc[...] = alpha * acc_sc[...] + jnp.dot(p.astype(BF16), v, preferred_element_type=F32)
        m_sc[...] = m_new
        return carry

    lax.fori_loop(0, nk, body, 0)
    o_ref[0, 0] = (acc_sc[...] / l_sc[...]).astype(o_ref.dtype)


def _flash_attn(q, k, v, *, tq, tk):
    b, h, s, dp = q.shape
    kv_spec = pl.BlockSpec((1, 1, s, dp), lambda bi, hi, qi: (bi, hi, 0, 0))
    return pl.pallas_call(
        functools.partial(_flash_kernel, tk=tk, nk=s // tk),
        grid=(b, h, s // tq),
        in_specs=[pl.BlockSpec((1, 1, tq, dp), lambda bi, hi, qi: (bi, hi, qi, 0)),
                  kv_spec, kv_spec],
        out_specs=pl.BlockSpec((1, 1, tq, dp), lambda bi, hi, qi: (bi, hi, qi, 0)),
        out_shape=jax.ShapeDtypeStruct((b, h, s, dp), BF16),
        scratch_shapes=[pltpu.VMEM((tq, 1), F32), pltpu.VMEM((tq, 1), F32),
                        pltpu.VMEM((tq, dp), F32)],
        compiler_params=pltpu.CompilerParams(
            dimension_semantics=("arbitrary", "arbitrary", "arbitrary"),
            vmem_limit_bytes=VMEM_LIMIT_BYTES),
        name="flash_attn",
    )(q, k, v)


def _s5_kernel(u_ref, t_ref, ws_ref, wof_ref, wob_ref, dre_ref, dim_ref, z_ref,
               sre_sc, sim_sc, xfre_sc, xfim_sc, xbre_sc, xbim_sc, *, gb, nk, nb):
    w = 2 * SSM_STATE
    for g in range(gb):
        s = jnp.dot(u_ref[g], ws_ref[g], preferred_element_type=F32)
        sre_sc[:, g * w:(g + 1) * w] = s[:, :w]
        sim_sc[:, g * w:(g + 1) * w] = s[:, w:]

    shape = (SUBLANES, gb * w)
    is_fwd = (lax.broadcasted_iota(jnp.int32, shape, 1) % w) < SSM_STATE
    top = lax.broadcasted_iota(jnp.int32, shape, 0) < nb
    d_re = jnp.broadcast_to(dre_ref[0], shape)
    d_im = jnp.broadcast_to(dim_ref[0], shape)
    swap = lambda a: pltpu.roll(a, nb, 0)

    def step(j, carry):
        re0, im0 = carry
        rf = pl.multiple_of(j * SUBLANES, SUBLANES)
        rb = pl.multiple_of((nk // 2 - 1 - j) * SUBLANES, SUBLANES)
        m_re = jnp.where(is_fwd, sre_sc[pl.ds(rf, SUBLANES), :], swap(sre_sc[pl.ds(rb, SUBLANES), :]))
        m_im = jnp.where(is_fwd, sim_sc[pl.ds(rf, SUBLANES), :], swap(sim_sc[pl.ds(rb, SUBLANES), :]))
        ms_re, ms_im = swap(m_re), swap(m_im)
        re1 = d_re * re0 - d_im * im0 + jnp.where(top, m_re, ms_re)
        im1 = d_re * im0 + d_im * re0 + jnp.where(top, m_im, ms_im)
        re2 = d_re * re1 - d_im * im1 + jnp.where(top, ms_re, m_re)
        im2 = d_re * im1 + d_im * re1 + jnp.where(top, ms_im, m_im)
        xfre_sc[pl.ds(rf, SUBLANES), :] = jnp.where(top, re0, re1)
        xfim_sc[pl.ds(rf, SUBLANES), :] = jnp.where(top, im0, im1)
        xbre_sc[pl.ds(rb, SUBLANES), :] = jnp.where(top, re1, re0)
        xbim_sc[pl.ds(rb, SUBLANES), :] = jnp.where(top, im1, im0)
        return re2, im2

    zero = jnp.zeros(shape, F32)
    lax.fori_loop(0, nk // 2, step, (zero, zero))

    for g in range(gb):
        sl = slice(g * w, (g + 1) * w)
        xf = jnp.concatenate([xfre_sc[:, sl], xfim_sc[:, sl]], axis=1).astype(BF16)
        xb = jnp.concatenate([xbre_sc[:, sl], xbim_sc[:, sl]], axis=1).astype(BF16)
        y = (jnp.dot(u_ref[g], t_ref[g], preferred_element_type=F32)
             + jnp.dot(xf, wof_ref[g], preferred_element_type=F32)
             + jnp.dot(xb, wob_ref[g], preferred_element_type=F32))
        z_ref[g] = jax.nn.gelu(y).astype(z_ref.dtype)


def _s5_scan(u5, t_op, ws, wof, wob, d_re, d_im, *, nk, nb, gb):
    g, r, cw = u5.shape
    w = 2 * SSM_STATE
    grp = lambda i: (i, 0, 0)
    return pl.pallas_call(
        functools.partial(_s5_kernel, gb=gb, nk=nk, nb=nb),
        grid=(g // gb,),
        in_specs=[pl.BlockSpec((gb, r, cw), grp),
                  pl.BlockSpec((gb, cw, cw), grp), pl.BlockSpec((gb, cw, 2 * w), grp),
                  pl.BlockSpec((gb, 2 * w, cw), grp), pl.BlockSpec((gb, 2 * w, cw), grp),
                  pl.BlockSpec((1, 1, gb * w), grp), pl.BlockSpec((1, 1, gb * w), grp)],
        out_specs=pl.BlockSpec((gb, r, cw), grp),
        out_shape=jax.ShapeDtypeStruct((g, r, cw), BF16),
        scratch_shapes=[pltpu.VMEM((r, gb * w), F32) for _ in range(6)],
        compiler_params=pltpu.CompilerParams(
            dimension_semantics=("arbitrary",), vmem_limit_bytes=VMEM_LIMIT_BYTES),
        name="s5_scan",
    )(u5, t_op, ws, wof, wob, d_re, d_im)


def _s5_operators(fwd, bwd, d_skip):
    L = S5_CHUNK

    def disc(lam_re, lam_im, log_dt, b_re, b_im, c_re, c_im):
        lam = lax.complex(lam_re.astype(F32), lam_im.astype(F32))
        lam_dt = lam * jnp.exp(log_dt.astype(F32))[:, None]
        b = lax.complex(b_re.astype(F32), b_im.astype(F32))
        b_bar = ((jnp.exp(lam_dt) - 1.0) / lam)[..., None] * b
        c = lax.complex(c_re.astype(F32), c_im.astype(F32))
        pw = jnp.exp(lam_dt[..., None] * jnp.arange(L + 1, dtype=F32))
        taps = jnp.einsum('gcp,gpt,gpd->gtcd', c, pw[..., :L], b_bar).real
        return pw, b_bar, c, taps

    pw_f, bb_f, c_f, taps_f = disc(*fwd)
    pw_b, bb_b, c_b, taps_b = disc(*bwd)
    G, P, C = bb_f.shape
    s_idx = jnp.arange(L)[:, None]
    t_idx = jnp.arange(L)[None, :]
    lag = t_idx - s_idx
    tf = jnp.where((lag >= 0)[None, :, :, None, None], taps_f[:, jnp.clip(lag, 0, L - 1)], 0.0)
    tb = jnp.where((lag <= 0)[None, :, :, None, None], taps_b[:, jnp.clip(-lag, 0, L - 1)], 0.0)
    skip = ((lag == 0)[None, :, :, None, None]
            * (jnp.eye(C, dtype=F32)[None, None, None] * d_skip.astype(F32)[:, None, None, :, None]))
    t_op = (tf + tb + skip).transpose(0, 1, 4, 2, 3).reshape(G, L * C, L * C)

    ws_f = jnp.einsum('gps,gpd->gsdp', pw_f[..., L - 1::-1][..., :L], bb_f)
    ws_b = jnp.einsum('gps,gpd->gsdp', pw_b[..., :L], bb_b)
    ws = jnp.concatenate([ws_f.real, ws_b.real, ws_f.imag, ws_b.imag], axis=-1).reshape(G, L * C, 4 * P)

    o_f = jnp.einsum('gcp,gpt->gptc', c_f, pw_f[..., 1:L + 1]).reshape(G, P, L * C)
    o_b = jnp.einsum('gcp,gpt->gptc', c_b, pw_b[..., L:0:-1]).reshape(G, P, L * C)
    zp = jnp.zeros((G, P, L * C), F32)
    wof = jnp.concatenate([o_f.real, zp, -o_f.imag, zp], axis=1)
    wob = jnp.concatenate([zp, o_b.real, zp, -o_b.imag], axis=1)

    d = jnp.concatenate([pw_f[..., L], pw_b[..., L]], axis=-1)
    return (t_op.astype(BF16), ws.astype(BF16), wof.astype(BF16), wob.astype(BF16),
            d.real.astype(F32), d.imag.astype(F32))


def _merge_ln_kernel(x_ref, o_ref, z_ref, sa_ref, sb_ref, woa_ref, wglu_ref, wos_ref, wout_ref,
                     g_ref, b_ref, out_ref):
    oa = jnp.concatenate(
        [o_ref[0, 2 * j] + o_ref[0, 2 * j + 1] for j in range(MLA_HEADS // 2)], axis=1)
    y_a = jnp.dot(oa, woa_ref[...], preferred_element_type=F32)
    z = z_ref[...]
    gl = jax.nn.sigmoid(jnp.dot(z, wglu_ref[...], preferred_element_type=F32))
    y_b = jnp.dot((z.astype(F32) * gl).astype(BF16), wos_ref[...], preferred_element_type=F32)
    merged = (sa_ref[...].astype(F32) * y_a + sb_ref[...].astype(F32) * y_b).astype(BF16)
    r = ALPHA * x_ref[...] + jnp.dot(merged, wout_ref[...], preferred_element_type=F32)
    out_ref[...] = _layer_norm(r, g_ref[...], b_ref[...])


def _merge_ln(x1, o, z, sa, sb, woa, wglu, wos, wout, g, b, *, seq, tm):
    n, d = x1.shape
    sw = z.shape[1]
    spb = seq // tm
    return pl.pallas_call(
        _merge_ln_kernel,
        grid=(n // tm,),
        in_specs=[pl.BlockSpec((tm, d), lambda i: (i, 0)),
                  pl.BlockSpec((1, MLA_HEADS, tm, HEAD_PAD), lambda i: (i // spb, 0, i % spb, 0)),
                  pl.BlockSpec((tm, sw), lambda i: (i, 0)),
                  pl.BlockSpec((tm, d), lambda i: (i, 0)),
                  pl.BlockSpec((tm, d), lambda i: (i, 0)),
                  _const_spec(woa.shape), _const_spec(wglu.shape), _const_spec(wos.shape),
                  _const_spec(wout.shape), _const_spec((1, d)), _const_spec((1, d))],
        out_specs=pl.BlockSpec((tm, d), lambda i: (i, 0)),
        out_shape=jax.ShapeDtypeStruct((n, d), F32),
        compiler_params=pltpu.CompilerParams(
            dimension_semantics=("arbitrary",), vmem_limit_bytes=VMEM_LIMIT_BYTES),
        name="merge_ln",
    )(x1, o, z, sa, sb, woa, wglu, wos, wout, g, b)


def _rot_cols(w):
    half = w.shape[-1] // 2
    return jnp.concatenate([-w[..., half:], w[..., :half]], axis=-1)


def _pad_heads(w, offset):
    r, h, dh = w.shape
    cols = [jnp.zeros((r, HEAD_PAD), w.dtype).at[:, offset(hi):offset(hi) + dh].set(w[:, hi])
            for hi in range(h)]
    return jnp.concatenate(cols, axis=1)


def kernel(x, p, ffn1_w1, ffn1_w3, ffn1_w2, ln1_g, ln1_b, w_in, q_norm_g, kv_norm_g, w_uq, w_uk, w_uv, w_o_attn, ssm_lam_re_f, ssm_lam_im_f, ssm_log_dt_f, ssm_b_re_f, ssm_b_im_f, ssm_c_re_f, ssm_c_im_f, ssm_lam_re_b, ssm_lam_im_b, ssm_log_dt_b, ssm_b_re_b, ssm_b_im_b, ssm_c_re_b, ssm_c_im_b, ssm_d, w_glu, w_o_ssm, w_out, ln2_g, ln2_b, ffn2_w1, ffn2_w3, ffn2_w2, ln3_g, ln3_b, ple_w_proj, ple_w_gate, ln4_g, ln4_b):
    batch, seq, d = x.shape
    n = batch * seq
    ssm_w = w_glu.shape[1]
    groups = ssm_w // SSM_GROUP
    assert DEPTH == 1 and ffn1_w1.shape[0] == 1
    assert seq % (2 * S5_CHUNK) == 0 and groups % S5_GROUP_BLOCK == 0
    assert 2 * batch == SUBLANES, "S5 scan packs two chunks of all batches per sublane tile"
    tm = min(512, seq)
    tq = min(512, seq)
    tk = min(512, seq)
    i = 0
    row = lambda a: a[i].reshape(1, -1).astype(F32)

    x0 = x.reshape(n, d)
    x1 = _ffn_ln(x0, ffn1_w1[i].astype(BF16), ffn1_w3[i].astype(BF16), ffn1_w2[i].astype(BF16),
                 row(ln1_g), row(ln1_b), tm=tm)

    o_kv = Q_LORA
    o_kr = o_kv + KV_LORA
    o_u = o_kr + QK_ROPE
    o_ga = o_u + ssm_w
    o_gb = o_ga + d
    w = w_in[i]
    w_kr = w[:, o_kr:o_u]
    kr_block = jnp.concatenate(
        [w_kr, _rot_cols(w_kr), jnp.zeros((d, LANES - 2 * QK_ROPE), w.dtype)], axis=1)
    win_ext = jnp.concatenate(
        [w[:, :o_kr], w[:, o_u:o_ga], w[:, o_ga:o_gb], w[:, o_gb:], kr_block], axis=1).astype(BF16)
    wq = w_uq[i]
    wq_rot = jnp.concatenate(
        [jnp.zeros_like(wq[..., :QK_NOPE]), _rot_cols(wq[..., QK_NOPE:])], axis=-1)
    wq_p = _pad_heads(wq, lambda h: 0).astype(BF16)
    wqr_p = _pad_heads(wq_rot, lambda h: 0).astype(BF16)
    wk_p = _pad_heads(w_uk[i], lambda h: 0).astype(BF16)
    wv_p = _pad_heads(w_uv[i], lambda h: (h % 2) * V_HEAD).astype(BF16)
    j = jnp.arange(QK_ROPE)
    e2 = jnp.zeros((LANES, MLA_HEADS, HEAD_PAD), F32)
    e2 = e2.at[j, :, QK_NOPE + j].set(1.0).at[QK_ROPE + j, :, QK_NOPE + j].set(1.0)
    e2 = e2.reshape(LANES, MLA_HEADS * HEAD_PAD).astype(BF16)
    pos = jnp.arange(seq, dtype=F32)
    inv_freq = ROPE_THETA ** (-jnp.arange(0, QK_ROPE, 2, dtype=F32) / QK_ROPE)
    ang = pos[:, None] * inv_freq[None, :]
    cos, sin = jnp.cos(ang), jnp.sin(ang)
    scale = (QK_NOPE + QK_ROPE) ** -0.5
    pad_q = jnp.zeros((seq, HEAD_PAD - QK_NOPE - QK_ROPE), F32)
    cosq = jnp.concatenate([jnp.ones((seq, QK_NOPE), F32), cos, cos, pad_q], axis=1) * scale
    sinq = jnp.concatenate([jnp.zeros((seq, QK_NOPE), F32), sin, sin, pad_q], axis=1) * scale
    tkr = jnp.concatenate([cos, cos, sin, sin, jnp.zeros((seq, LANES - 2 * QK_ROPE), F32)], axis=1)

    q, k, v, u, sa, sb = _proj_mla(
        x1, win_ext, row(q_norm_g), row(kv_norm_g), wq_p, wqr_p, wk_p, wv_p, e2, cosq, sinq, tkr,
        batch=batch, seq=seq, tm=tm, ssm_w=ssm_w)

    o = _flash_attn(q, k, v, tq=tq, tk=tk)

    nk = seq // S5_CHUNK
    fwd = (ssm_lam_re_f[i], ssm_lam_im_f[i], ssm_log_dt_f[i], ssm_b_re_f[i], ssm_b_im_f[i],
           ssm_c_re_f[i], ssm_c_im_f[i])
    bwd = (ssm_lam_re_b[i], ssm_lam_im_b[i], ssm_log_dt_b[i], ssm_b_re_b[i], ssm_b_im_b[i],
           ssm_c_re_b[i], ssm_c_im_b[i])
    t_op, ws, wof, wob, d_re, d_im = _s5_operators(fwd, bwd, ssm_d[i].reshape(groups, SSM_GROUP))
    gb = S5_GROUP_BLOCK
    d_re = d_re.reshape(groups // gb, 1, gb * 2 * SSM_STATE)
    d_im = d_im.reshape(groups // gb, 1, gb * 2 * SSM_STATE)
    u5 = (u.reshape(batch, nk, S5_CHUNK, groups, SSM_GROUP).transpose(3, 1, 0, 2, 4)
          .reshape(groups, nk * batch, S5_CHUNK * SSM_GROUP))
    z5 = _s5_scan(u5, t_op, ws, wof, wob, d_re, d_im, nk=nk, nb=batch, gb=gb)
    z = (z5.reshape(groups, nk, batch, S5_CHUNK, SSM_GROUP).transpose(2, 1, 3, 0, 4)
         .reshape(n, ssm_w))

    x2 = _merge_ln(x1, o, z, sa, sb, w_o_attn[i].astype(BF16), w_glu[i].astype(BF16),
                   w_o_ssm[i].astype(BF16), w_out[i].astype(BF16), row(ln2_g), row(ln2_b),
                   seq=seq, tm=tm)

    x4 = _ffn_ple_ln(x2, p[i].reshape(n, -1), ffn2_w1[i].astype(BF16), ffn2_w3[i].astype(BF16),
                     ffn2_w2[i].astype(BF16), row(ln3_g), row(ln3_b),
                     ple_w_gate[i].astype(BF16), ple_w_proj[i].astype(BF16),
                     row(ln4_g), row(ln4_b), tm=tm)
    return x4.reshape(batch, seq, d)
```

```python
import functools
import math

import jax
import jax.numpy as jnp
from jax import lax
from jax.experimental import pallas as pl
from jax.experimental.pallas import tpu as pltpu

F32 = jnp.float32
BF16 = jnp.bfloat16

DEPTH = 1
MLA_HEADS = 8
QK_NOPE = 64
QK_ROPE = 32
V_HEAD = 64
Q_LORA = 384
KV_LORA = 256
ROPE_THETA = 10000.0
SSM_GROUP = 16
SSM_STATE = 64
LN_EPS = 1e-5
RMS_EPS = 1e-6
ALPHA = (2.0 * DEPTH) ** 0.25

LANES = 128
SUBLANES = 8
VMEM_LIMIT_BYTES = 56 * 1024 * 1024

HEAD_PAD = LANES
S5_CHUNK = 16
S5_GROUP_BLOCK = 4


def _const_spec(shape):
    nd = len(shape)
    return pl.BlockSpec(shape, lambda *_: (0,) * nd, pipeline_mode=pl.Buffered(1))


def _layer_norm(r, g, b):
    mu = jnp.mean(r, axis=-1, keepdims=True)
    c = r - mu
    var = jnp.mean(c * c, axis=-1, keepdims=True)
    return c * lax.rsqrt(var + LN_EPS) * g + b


def _rms_norm(x, g):
    return x * lax.rsqrt(jnp.mean(x * x, axis=-1, keepdims=True) + RMS_EPS) * g


def _swiglu(xb, w1_ref, w3_ref, w2_ref):
    h1 = jnp.dot(xb, w1_ref[...], preferred_element_type=F32)
    h3 = jnp.dot(xb, w3_ref[...], preferred_element_type=F32)
    a = (h1 * jax.nn.sigmoid(h1) * h3).astype(BF16)
    return jnp.dot(a, w2_ref[...], preferred_element_type=F32)


def _ffn_ln_kernel(x_ref, w1_ref, w3_ref, w2_ref, g_ref, b_ref, o_ref):
    x = x_ref[...]
    y = _swiglu(x.astype(BF16), w1_ref, w3_ref, w2_ref)
    o_ref[...] = _layer_norm(ALPHA * x + 0.5 * y, g_ref[...], b_ref[...])


def _ffn_ln(x, w1, w3, w2, g, b, *, tm):
    n, d = x.shape
    f = w1.shape[1]
    return pl.pallas_call(
        _ffn_ln_kernel,
        grid=(n // tm,),
        in_specs=[pl.BlockSpec((tm, d), lambda i: (i, 0)),
                  _const_spec((d, f)), _const_spec((d, f)), _const_spec((f, d)),
                  _const_spec((1, d)), _const_spec((1, d))],
        out_specs=pl.BlockSpec((tm, d), lambda i: (i, 0)),
        out_shape=jax.ShapeDtypeStruct((n, d), F32),
        compiler_params=pltpu.CompilerParams(
            dimension_semantics=("arbitrary",), vmem_limit_bytes=VMEM_LIMIT_BYTES),
        name="ffn_ln",
    )(x, w1, w3, w2, g, b)


def _ffn_ple_ln_kernel(x_ref, p_ref, w1_ref, w3_ref, w2_ref, g3_ref, b3_ref,
                       wpg_ref, wpp_ref, g4_ref, b4_ref, o_ref):
    x = x_ref[...]
    y = _swiglu(x.astype(BF16), w1_ref, w3_ref, w2_ref)
    x3 = _layer_norm(ALPHA * x + 0.5 * y, g3_ref[...], b3_ref[...])
    gate = jax.nn.sigmoid(jnp.dot(x3.astype(BF16), wpg_ref[...], preferred_element_type=F32))
    emb = jnp.dot(p_ref[...].astype(BF16), wpp_ref[...], preferred_element_type=F32)
    o_ref[...] = _layer_norm(ALPHA * x3 + gate * emb, g4_ref[...], b4_ref[...])


def _ffn_ple_ln(x, p, w1, w3, w2, g3, b3, wpg, wpp, g4, b4, *, tm):
    n, d = x.shape
    f = w1.shape[1]
    pd = p.shape[1]
    return pl.pallas_call(
        _ffn_ple_ln_kernel,
        grid=(n // tm,),
        in_specs=[pl.BlockSpec((tm, d), lambda i: (i, 0)),
                  pl.BlockSpec((tm, pd), lambda i: (i, 0)),
                  _const_spec((d, f)), _const_spec((d, f)), _const_spec((f, d)),
                  _const_spec((1, d)), _const_spec((1, d)),
                  _const_spec((d, d)), _const_spec((pd, d)),
                  _const_spec((1, d)), _const_spec((1, d))],
        out_specs=pl.BlockSpec((tm, d), lambda i: (i, 0)),
        out_shape=jax.ShapeDtypeStruct((n, d), F32),
        compiler_params=pltpu.CompilerParams(
            dimension_semantics=("arbitrary",), vmem_limit_bytes=VMEM_LIMIT_BYTES),
        name="ffn_ple_ln",
    )(x, p, w1, w3, w2, g3, b3, wpg, wpp, g4, b4)


def _proj_mla_kernel(x_ref, win_ref, gq_ref, gkv_ref, wq_ref, wqr_ref, wk_ref, wv_ref,
                     e_ref, cosq_ref, sinq_ref, tk_ref,
                     q_ref, k_ref, vt_ref, u_ref, sa_ref, sb_ref, *, d, ssm_w):
    xb = x_ref[...].astype(BF16)
    proj = jnp.dot(xb, win_ref[...], preferred_element_type=F32)
    o_kv = Q_LORA
    o_u = o_kv + KV_LORA
    o_ga = o_u + ssm_w
    o_gb = o_ga + d
    o_kr = o_gb + d
    c_q = _rms_norm(proj[:, :Q_LORA], gq_ref[...]).astype(BF16)
    c_kv = _rms_norm(proj[:, o_kv:o_u], gkv_ref[...]).astype(BF16)
    u_ref[...] = proj[:, o_u:o_ga].astype(BF16)
    sa_ref[...] = jax.nn.sigmoid(proj[:, o_ga:o_gb]).astype(BF16)
    sb_ref[...] = jax.nn.sigmoid(proj[:, o_gb:o_kr]).astype(BF16)
    kr = (proj[:, o_kr:o_kr + LANES] * tk_ref[...]).astype(BF16)
    q = jnp.dot(c_q, wq_ref[...], preferred_element_type=F32)
    q_rot = jnp.dot(c_q, wqr_ref[...], preferred_element_type=F32)
    k = (jnp.dot(c_kv, wk_ref[...], preferred_element_type=F32)
         + jnp.dot(kr, e_ref[...], preferred_element_type=F32))
    vt = lax.dot_general(wv_ref[...], c_kv, (((1,), (1,)), ((), ())), preferred_element_type=F32)
    cosq = cosq_ref[...]
    sinq = sinq_ref[...]
    for h in range(MLA_HEADS):
        sl = slice(h * HEAD_PAD, (h + 1) * HEAD_PAD)
        q_ref[0, h] = (q[:, sl] * cosq + q_rot[:, sl] * sinq).astype(BF16)
        k_ref[0, h] = k[:, sl].astype(BF16)
        vt_ref[0, h, 0] = vt[sl, :].astype(BF16)


def _proj_mla(x1, win, gq, gkv, wq, wqr, wk, wv, e2, cosq, sinq, tkr, *, batch, seq, tm, ssm_w):
    n, d = x1.shape
    wcols = win.shape[1]
    hp = MLA_HEADS * HEAD_PAD
    spb = seq // tm
    head_spec = pl.BlockSpec((1, MLA_HEADS, tm, HEAD_PAD), lambda i: (i // spb, 0, i % spb, 0))
    head_shape = jax.ShapeDtypeStruct((batch, MLA_HEADS, seq, HEAD_PAD), BF16)
    vt_spec = pl.BlockSpec((1, MLA_HEADS, 1, HEAD_PAD, tm), lambda i: (i // spb, 0, i % spb, 0, 0))
    vt_shape = jax.ShapeDtypeStruct((batch, MLA_HEADS, spb, HEAD_PAD, tm), BF16)
    tab_spec = pl.BlockSpec((tm, LANES), lambda i: (i % spb, 0))
    return pl.pallas_call(
        functools.partial(_proj_mla_kernel, d=d, ssm_w=ssm_w),
        grid=(n // tm,),
        in_specs=[pl.BlockSpec((tm, d), lambda i: (i, 0)),
                  _const_spec((d, wcols)),
                  _const_spec((1, Q_LORA)), _const_spec((1, KV_LORA)),
                  _const_spec((Q_LORA, hp)), _const_spec((Q_LORA, hp)),
                  _const_spec((KV_LORA, hp)), _const_spec((hp, KV_LORA)),
                  _const_spec((LANES, hp)),
                  tab_spec, tab_spec, tab_spec],
        out_specs=[head_spec, head_spec, vt_spec,
                   pl.BlockSpec((tm, ssm_w), lambda i: (i, 0)),
                   pl.BlockSpec((tm, d), lambda i: (i, 0)),
                   pl.BlockSpec((tm, d), lambda i: (i, 0))],
        out_shape=[head_shape, head_shape, vt_shape,
                   jax.ShapeDtypeStruct((n, ssm_w), BF16),
                   jax.ShapeDtypeStruct((n, d), BF16),
                   jax.ShapeDtypeStruct((n, d), BF16)],
        compiler_params=pltpu.CompilerParams(
            dimension_semantics=("arbitrary",), vmem_limit_bytes=VMEM_LIMIT_BYTES),
        name="proj_mla",
    )(x1, win, gq, gkv, wq, wqr, wk, wv, e2, cosq, sinq, tkr)


def _flash_kernel(q_ref, k_ref, vt_ref, o_ref, m_sc, l_sc, acc_sc, *, tk, nk):
    q = q_ref[0, 0]
    m_sc[...] = jnp.full(m_sc.shape, -jnp.inf, F32)
    l_sc[...] = jnp.zeros(l_sc.shape, F32)
    acc_sc[...] = jnp.zeros(acc_sc.shape, F32)

    def body(j, carry):
        off = pl.multiple_of(j * tk, tk)
        k = k_ref[0, 0, pl.ds(off, tk), :]
        s = lax.dot_general(k, q, (((1,), (1,)), ((), ())), preferred_element_type=F32)
        m_prev = m_sc[...]
        m_new = jnp.maximum(m_prev, jnp.max(s, axis=0, keepdims=True))
        alpha = jnp.exp(m_prev - m_new)
        p = jnp.exp(s - m_new)
        l_sc[...] = alpha * l_sc[...] + jnp.sum(p, axis=0, keepdims=True)
        acc_sc[...] = alpha * acc_sc[...] + jnp.dot(
            vt_ref[0, 0, j], p.astype(BF16), preferred_element_type=F32)
        m_sc[...] = m_new
        return carry

    lax.fori_loop(0, nk, body, 0, unroll=2)
    o_ref[0, 0] = (acc_sc[...] / l_sc[...]).T.astype(o_ref.dtype)


def _flash_attn(q, k, vt, *, tq):
    b, h, s, dp = q.shape
    nk, tk = vt.shape[2], vt.shape[4]
    return pl.pallas_call(
        functools.partial(_flash_kernel, tk=tk, nk=nk),
        grid=(b, h, s // tq),
        in_specs=[pl.BlockSpec((1, 1, tq, dp), lambda bi, hi, qi: (bi, hi, qi, 0)),
                  pl.BlockSpec((1, 1, s, dp), lambda bi, hi, qi: (bi, hi, 0, 0)),
                  pl.BlockSpec((1, 1, nk, dp, tk), lambda bi, hi, qi: (bi, hi, 0, 0, 0))],
        out_specs=pl.BlockSpec((1, 1, tq, dp), lambda bi, hi, qi: (bi, hi, qi, 0)),
        out_shape=jax.ShapeDtypeStruct((b, h, s, dp), BF16),
        scratch_shapes=[pltpu.VMEM((1, tq), F32), pltpu.VMEM((1, tq), F32),
                        pltpu.VMEM((dp, tq), F32)],
        compiler_params=pltpu.CompilerParams(
            dimension_semantics=("arbitrary", "arbitrary", "arbitrary"),
            vmem_limit_bytes=VMEM_LIMIT_BYTES),
        name="flash_attn",
    )(q, k, vt)


def _s5_kernel(u_ref, t_ref, ws_ref, wof_ref, wob_ref, dre_ref, dim_ref, z_ref,
               sre_sc, sim_sc, xfre_sc, xfim_sc, xbre_sc, xbim_sc, *, gb, nk, nb):
    w = 2 * SSM_STATE
    for g in range(gb):
        s = jnp.dot(u_ref[g], ws_ref[g], preferred_element_type=F32)
        sre_sc[:, g * w:(g + 1) * w] = s[:, :w]
        sim_sc[:, g * w:(g + 1) * w] = s[:, w:]

    shape = (SUBLANES, gb * w)
    is_fwd = (lax.broadcasted_iota(jnp.int32, shape, 1) % w) < SSM_STATE
    top = lax.broadcasted_iota(jnp.int32, shape, 0) < nb
    d_re = jnp.broadcast_to(dre_ref[0], shape)
    d_im = jnp.broadcast_to(dim_ref[0], shape)
    swap = lambda a: pltpu.roll(a, nb, 0)

    def step(j, carry):
        re0, im0 = carry
        rf = pl.multiple_of(j * SUBLANES, SUBLANES)
        rb = pl.multiple_of((nk // 2 - 1 - j) * SUBLANES, SUBLANES)
        m_re = jnp.where(is_fwd, sre_sc[pl.ds(rf, SUBLANES), :], swap(sre_sc[pl.ds(rb, SUBLANES), :]))
        m_im = jnp.where(is_fwd, sim_sc[pl.ds(rf, SUBLANES), :], swap(sim_sc[pl.ds(rb, SUBLANES), :]))
        ms_re, ms_im = swap(m_re), swap(m_im)
        re1 = d_re * re0 - d_im * im0 + jnp.where(top, m_re, ms_re)
        im1 = d_re * im0 + d_im * re0 + jnp.where(top, m_im, ms_im)
        re2 = d_re * re1 - d_im * im1 + jnp.where(top, ms_re, m_re)
        im2 = d_re * im1 + d_im * re1 + jnp.where(top, ms_im, m_im)
        xfre_sc[pl.ds(rf, SUBLANES), :] = jnp.where(top, re0, re1)
        xfim_sc[pl.ds(rf, SUBLANES), :] = jnp.where(top, im0, im1)
        xbre_sc[pl.ds(rb, SUBLANES), :] = jnp.where(top, re1, re0)
        xbim_sc[pl.ds(rb, SUBLANES), :] = jnp.where(top, im1, im0)
        return re2, im2

    zero = jnp.zeros(shape, F32)
    lax.fori_loop(0, nk // 2, step, (zero, zero))

    for g in range(gb):
        sl = slice(g * w, (g + 1) * w)
        xf = jnp.concatenate([xfre_sc[:, sl], xfim_sc[:, sl]], axis=1).astype(BF16)
        xb = jnp.concatenate([xbre_sc[:, sl], xbim_sc[:, sl]], axis=1).astype(BF16)
        y = (jnp.dot(u_ref[g], t_ref[g], preferred_element_type=F32)
             + jnp.dot(xf, wof_ref[g], preferred_element_type=F32)
             + jnp.dot(xb, wob_ref[g], preferred_element_type=F32))
        z_ref[g] = jax.nn.gelu(y).astype(z_ref.dtype)


def _s5_scan(u5, t_op, ws, wof, wob, d_re, d_im, *, nk, nb, gb):
    g, r, cw = u5.shape
    w = 2 * SSM_STATE
    grp = lambda i: (i, 0, 0)
    return pl.pallas_call(
        functools.partial(_s5_kernel, gb=gb, nk=nk, nb=nb),
        grid=(g // gb,),
        in_specs=[pl.BlockSpec((gb, r, cw), grp),
                  pl.BlockSpec((gb, cw, cw), grp), pl.BlockSpec((gb, cw, 2 * w), grp),
                  pl.BlockSpec((gb, 2 * w, cw), grp), pl.BlockSpec((gb, 2 * w, cw), grp),
                  pl.BlockSpec((1, 1, gb * w), grp), pl.BlockSpec((1, 1, gb * w), grp)],
        out_specs=pl.BlockSpec((gb, r, cw), grp),
        out_shape=jax.ShapeDtypeStruct((g, r, cw), BF16),
        scratch_shapes=[pltpu.VMEM((r, gb * w), F32) for _ in range(6)],
        compiler_params=pltpu.CompilerParams(
            dimension_semantics=("arbitrary",), vmem_limit_bytes=VMEM_LIMIT_BYTES),
        name="s5_scan",
    )(u5, t_op, ws, wof, wob, d_re, d_im)


def _s5_operators(fwd, bwd, d_skip):
    L = S5_CHUNK
    hi = lax.Precision.HIGHEST

    def disc(lam_re, lam_im, log_dt, b_re, b_im, c_re, c_im):
        lr, li = lam_re.astype(F32), lam_im.astype(F32)
        dt = jnp.exp(log_dt.astype(F32))[:, None]
        steps = jnp.arange(L + 1, dtype=F32)
        mag = jnp.exp((lr * dt)[..., None] * steps)
        ang = (li * dt)[..., None] * steps
        pw_re, pw_im = mag * jnp.cos(ang), mag * jnp.sin(ang)
        xr, xi = pw_re[..., 1] - 1.0, pw_im[..., 1]
        den = lr * lr + li * li
        kr, ki = (xr * lr + xi * li) / den, (xi * lr - xr * li) / den
        br, bi = b_re.astype(F32), b_im.astype(F32)
        bb_re = kr[..., None] * br - ki[..., None] * bi
        bb_im = kr[..., None] * bi + ki[..., None] * br
        e_re = pw_re[..., :L, None] * bb_re[:, :, None, :] - pw_im[..., :L, None] * bb_im[:, :, None, :]
        e_im = pw_re[..., :L, None] * bb_im[:, :, None, :] + pw_im[..., :L, None] * bb_re[:, :, None, :]
        cr, ci = c_re.astype(F32), c_im.astype(F32)
        taps = (jnp.einsum('gcp,gptd->gtcd', cr, e_re, precision=hi)
                - jnp.einsum('gcp,gptd->gtcd', ci, e_im, precision=hi))
        return pw_re, pw_im, e_re, e_im, cr, ci, taps

    pwr_f, pwi_f, er_f, ei_f, cr_f, ci_f, taps_f = disc(*fwd)
    pwr_b, pwi_b, er_b, ei_b, cr_b, ci_b, taps_b = disc(*bwd)
    G, P = pwr_f.shape[:2]
    C = er_f.shape[-1]
    s_idx = jnp.arange(L)[:, None]
    t_idx = jnp.arange(L)[None, :]
    lag = t_idx - s_idx
    tf = jnp.where((lag >= 0)[None, :, :, None, None], taps_f[:, jnp.clip(lag, 0, L - 1)], 0.0)
    tb = jnp.where((lag <= 0)[None, :, :, None, None], taps_b[:, jnp.clip(-lag, 0, L - 1)], 0.0)
    skip = ((lag == 0)[None, :, :, None, None]
            * (jnp.eye(C, dtype=F32)[None, None, None] * d_skip.astype(F32)[:, None, None, :, None]))
    t_op = (tf + tb + skip).transpose(0, 1, 4, 2, 3).reshape(G, L * C, L * C)

    to_sdp = lambda e: e.transpose(0, 2, 3, 1)
    ws = jnp.concatenate(
        [to_sdp(er_f[:, :, ::-1]), to_sdp(er_b), to_sdp(ei_f[:, :, ::-1]), to_sdp(ei_b)],
        axis=-1).reshape(G, L * C, 4 * P)

    def out_map(cr, ci, pr, pi):
        c_r, c_i = cr.transpose(0, 2, 1)[:, :, None, :], ci.transpose(0, 2, 1)[:, :, None, :]
        p_r, p_i = pr[..., None], pi[..., None]
        return ((c_r * p_r - c_i * p_i).reshape(G, P, L * C),
                (c_r * p_i + c_i * p_r).reshape(G, P, L * C))

    of_re, of_im = out_map(cr_f, ci_f, pwr_f[..., 1:L + 1], pwi_f[..., 1:L + 1])
    ob_re, ob_im = out_map(cr_b, ci_b, pwr_b[..., L:0:-1], pwi_b[..., L:0:-1])
    zp = jnp.zeros((G, P, L * C), F32)
    wof = jnp.concatenate([of_re, zp, -of_im, zp], axis=1)
    wob = jnp.concatenate([zp, ob_re, zp, -ob_im], axis=1)

    d_re = jnp.concatenate([pwr_f[..., L], pwr_b[..., L]], axis=-1)
    d_im = jnp.concatenate([pwi_f[..., L], pwi_b[..., L]], axis=-1)
    return (t_op.astype(BF16), ws.astype(BF16), wof.astype(BF16), wob.astype(BF16), d_re, d_im)


def _merge_ln_kernel(x_ref, o_ref, z_ref, sa_ref, sb_ref, woa_ref, wglu_ref, wos_ref, wout_ref,
                     g_ref, b_ref, out_ref):
    oa = jnp.concatenate(
        [o_ref[0, 2 * j] + o_ref[0, 2 * j + 1] for j in range(MLA_HEADS // 2)], axis=1)
    y_a = jnp.dot(oa, woa_ref[...], preferred_element_type=F32)
    z = z_ref[...]
    gl = jax.nn.sigmoid(jnp.dot(z, wglu_ref[...], preferred_element_type=F32))
    y_b = jnp.dot((z.astype(F32) * gl).astype(BF16), wos_ref[...], preferred_element_type=F32)
    merged = (sa_ref[...].astype(F32) * y_a + sb_ref[...].astype(F32) * y_b).astype(BF16)
    r = ALPHA * x_ref[...] + jnp.dot(merged, wout_ref[...], preferred_element_type=F32)
    out_ref[...] = _layer_norm(r, g_ref[...], b_ref[...])


def _merge_ln(x1, o, z, sa, sb, woa, wglu, wos, wout, g, b, *, seq, tm):
    n, d = x1.shape
    sw = z.shape[1]
    spb = seq // tm
    return pl.pallas_call(
        _merge_ln_kernel,
        grid=(n // tm,),
        in_specs=[pl.BlockSpec((tm, d), lambda i: (i, 0)),
                  pl.BlockSpec((1, MLA_HEADS, tm, HEAD_PAD), lambda i: (i // spb, 0, i % spb, 0)),
                  pl.BlockSpec((tm, sw), lambda i: (i, 0)),
                  pl.BlockSpec((tm, d), lambda i: (i, 0)),
                  pl.BlockSpec((tm, d), lambda i: (i, 0)),
                  _const_spec(woa.shape), _const_spec(wglu.shape), _const_spec(wos.shape),
                  _const_spec(wout.shape), _const_spec((1, d)), _const_spec((1, d))],
        out_specs=pl.BlockSpec((tm, d), lambda i: (i, 0)),
        out_shape=jax.ShapeDtypeStruct((n, d), F32),
        compiler_params=pltpu.CompilerParams(
            dimension_semantics=("arbitrary",), vmem_limit_bytes=VMEM_LIMIT_BYTES),
        name="merge_ln",
    )(x1, o, z, sa, sb, woa, wglu, wos, wout, g, b)


def _rot_cols(w):
    half = w.shape[-1] // 2
    return jnp.concatenate([-w[..., half:], w[..., :half]], axis=-1)


def _pad_heads(w, offset):
    r, h, dh = w.shape
    cols = [jnp.zeros((r, HEAD_PAD), w.dtype).at[:, offset(hi):offset(hi) + dh].set(w[:, hi])
            for hi in range(h)]
    return jnp.concatenate(cols, axis=1)


def kernel(x, p, ffn1_w1, ffn1_w3, ffn1_w2, ln1_g, ln1_b, w_in, q_norm_g, kv_norm_g, w_uq, w_uk, w_uv, w_o_attn, ssm_lam_re_f, ssm_lam_im_f, ssm_log_dt_f, ssm_b_re_f, ssm_b_im_f, ssm_c_re_f, ssm_c_im_f, ssm_lam_re_b, ssm_lam_im_b, ssm_log_dt_b, ssm_b_re_b, ssm_b_im_b, ssm_c_re_b, ssm_c_im_b, ssm_d, w_glu, w_o_ssm, w_out, ln2_g, ln2_b, ffn2_w1, ffn2_w3, ffn2_w2, ln3_g, ln3_b, ple_w_proj, ple_w_gate, ln4_g, ln4_b):
    batch, seq, d = x.shape
    n = batch * seq
    ssm_w = w_glu.shape[1]
    groups = ssm_w // SSM_GROUP
    assert DEPTH == 1 and ffn1_w1.shape[0] == 1
    assert seq % (2 * S5_CHUNK) == 0 and groups % S5_GROUP_BLOCK == 0
    assert 2 * batch == SUBLANES, "S5 scan packs two chunks of all batches per sublane tile"
    tm = min(512, seq)
    tq = min(512, seq)
    i = 0
    row = lambda a: a[i].reshape(1, -1).astype(F32)

    x0 = x.reshape(n, d)
    x1 = _ffn_ln(x0, ffn1_w1[i].astype(BF16), ffn1_w3[i].astype(BF16), ffn1_w2[i].astype(BF16),
                 row(ln1_g), row(ln1_b), tm=tm)

    o_kv = Q_LORA
    o_kr = o_kv + KV_LORA
    o_u = o_kr + QK_ROPE
    o_ga = o_u + ssm_w
    o_gb = o_ga + d
    w = w_in[i]
    w_kr = w[:, o_kr:o_u]
    kr_block = jnp.concatenate(
        [w_kr, _rot_cols(w_kr), jnp.zeros((d, LANES - 2 * QK_ROPE), w.dtype)], axis=1)
    win_ext = jnp.concatenate(
        [w[:, :o_kr], w[:, o_u:o_ga], w[:, o_ga:o_gb], w[:, o_gb:], kr_block], axis=1).astype(BF16)
    wq = w_uq[i]
    wq_rot = jnp.concatenate(
        [jnp.zeros_like(wq[..., :QK_NOPE]), _rot_cols(wq[..., QK_NOPE:])], axis=-1)
    wq_p = _pad_heads(wq, lambda h: 0).astype(BF16)
    wqr_p = _pad_heads(wq_rot, lambda h: 0).astype(BF16)
    wk_p = _pad_heads(w_uk[i], lambda h: 0).astype(BF16)
    wv_pt = _pad_heads(w_uv[i], lambda h: (h % 2) * V_HEAD).T.astype(BF16)
    j = jnp.arange(QK_ROPE)
    e2 = jnp.zeros((LANES, MLA_HEADS, HEAD_PAD), F32)
    e2 = e2.at[j, :, QK_NOPE + j].set(1.0).at[QK_ROPE + j, :, QK_NOPE + j].set(1.0)
    e2 = e2.reshape(LANES, MLA_HEADS * HEAD_PAD).astype(BF16)
    pos = jnp.arange(seq, dtype=F32)
    inv_freq = ROPE_THETA ** (-jnp.arange(0, QK_ROPE, 2, dtype=F32) / QK_ROPE)
    ang = pos[:, None] * inv_freq[None, :]
    cos, sin = jnp.cos(ang), jnp.sin(ang)
    scale = (QK_NOPE + QK_ROPE) ** -0.5
    pad_q = jnp.zeros((seq, HEAD_PAD - QK_NOPE - QK_ROPE), F32)
    cosq = jnp.concatenate([jnp.ones((seq, QK_NOPE), F32), cos, cos, pad_q], axis=1) * scale
    sinq = jnp.concatenate([jnp.zeros((seq, QK_NOPE), F32), sin, sin, pad_q], axis=1) * scale
    tkr = jnp.concatenate([cos, cos, sin, sin, jnp.zeros((seq, LANES - 2 * QK_ROPE), F32)], axis=1)

    q, k, vt, u, sa, sb = _proj_mla(
        x1, win_ext, row(q_norm_g), row(kv_norm_g), wq_p, wqr_p, wk_p, wv_pt, e2, cosq, sinq, tkr,
        batch=batch, seq=seq, tm=tm, ssm_w=ssm_w)

    o = _flash_attn(q, k, vt, tq=tq)

    nk = seq // S5_CHUNK
    fwd = (ssm_lam_re_f[i], ssm_lam_im_f[i], ssm_log_dt_f[i], ssm_b_re_f[i], ssm_b_im_f[i],
           ssm_c_re_f[i], ssm_c_im_f[i])
    bwd = (ssm_lam_re_b[i], ssm_lam_im_b[i], ssm_log_dt_b[i], ssm_b_re_b[i], ssm_b_im_b[i],
           ssm_c_re_b[i], ssm_c_im_b[i])
    t_op, ws, wof, wob, d_re, d_im = _s5_operators(fwd, bwd, ssm_d[i].reshape(groups, SSM_GROUP))
    gb = S5_GROUP_BLOCK
    d_re = d_re.reshape(groups // gb, 1, gb * 2 * SSM_STATE)
    d_im = d_im.reshape(groups // gb, 1, gb * 2 * SSM_STATE)
    u5 = (u.reshape(batch, nk, S5_CHUNK, groups, SSM_GROUP).transpose(3, 1, 0, 2, 4)
          .reshape(groups, nk * batch, S5_CHUNK * SSM_GROUP))
    z5 = _s5_scan(u5, t_op, ws, wof, wob, d_re, d_im, nk=nk, nb=batch, gb=gb)
    z = (z5.reshape(groups, nk, batch, S5_CHUNK, SSM_GROUP).transpose(2, 1, 3, 0, 4)
         .reshape(n, ssm_w))

    x2 = _merge_ln(x1, o, z, sa, sb, w_o_attn[i].astype(BF16), w_glu[i].astype(BF16),
                   w_o_ssm[i].astype(BF16), w_out[i].astype(BF16), row(ln2_g), row(ln2_b),
                   seq=seq, tm=tm)

    x4 = _ffn_ple_ln(x2, p[i].reshape(n, -1), ffn2_w1[i].astype(BF16), ffn2_w3[i].astype(BF16),
                     ffn2_w2[i].astype(BF16), row(ln3_g), row(ln3_b),
                     ple_w_gate[i].astype(BF16), ple_w_proj[i].astype(BF16),
                     row(ln4_g), row(ln4_b), tm=tm)
    return x4.reshape(batch, seq, d)
```

```python
import functools
import math

import jax
import jax.numpy as jnp
from jax import lax
from jax.experimental import pallas as pl
from jax.experimental.pallas import tpu as pltpu

F32 = jnp.float32
BF16 = jnp.bfloat16

DEPTH = 1
MLA_HEADS = 8
QK_NOPE = 64
QK_ROPE = 32
V_HEAD = 64
Q_LORA = 384
KV_LORA = 256
ROPE_THETA = 10000.0
SSM_GROUP = 16
SSM_STATE = 64
LN_EPS = 1e-5
RMS_EPS = 1e-6
ALPHA = (2.0 * DEPTH) ** 0.25

LANES = 128
SUBLANES = 8
VMEM_LIMIT_BYTES = 56 * 1024 * 1024

HEAD_PAD = LANES
S5_CHUNK = 16
S5_GROUP_BLOCK = 4


def _const_spec(shape):
    nd = len(shape)
    return pl.BlockSpec(shape, lambda *_: (0,) * nd, pipeline_mode=pl.Buffered(1))


def _layer_norm(r, g, b):
    mu = jnp.mean(r, axis=-1, keepdims=True)
    c = r - mu
    var = jnp.mean(c * c, axis=-1, keepdims=True)
    return c * lax.rsqrt(var + LN_EPS) * g + b


def _rms_norm(x, g):
    return x * lax.rsqrt(jnp.mean(x * x, axis=-1, keepdims=True) + RMS_EPS) * g


def _swiglu(xb, w1_ref, w3_ref, w2_ref):
    h1 = jnp.dot(xb, w1_ref[...], preferred_element_type=F32)
    h3 = jnp.dot(xb, w3_ref[...], preferred_element_type=F32)
    a = (h1 * jax.nn.sigmoid(h1) * h3).astype(BF16)
    return jnp.dot(a, w2_ref[...], preferred_element_type=F32)


def _ffn_ln_kernel(x_ref, w1_ref, w3_ref, w2_ref, g_ref, b_ref, o_ref):
    x = x_ref[...]
    y = _swiglu(x.astype(BF16), w1_ref, w3_ref, w2_ref)
    o_ref[...] = _layer_norm(ALPHA * x + 0.5 * y, g_ref[...], b_ref[...])


def _ffn_ln(x, w1, w3, w2, g, b, *, tm):
    n, d = x.shape
    f = w1.shape[1]
    return pl.pallas_call(
        _ffn_ln_kernel,
        grid=(n // tm,),
        in_specs=[pl.BlockSpec((tm, d), lambda i: (i, 0)),
                  _const_spec((d, f)), _const_spec((d, f)), _const_spec((f, d)),
                  _const_spec((1, d)), _const_spec((1, d))],
        out_specs=pl.BlockSpec((tm, d), lambda i: (i, 0)),
        out_shape=jax.ShapeDtypeStruct((n, d), F32),
        compiler_params=pltpu.CompilerParams(
            dimension_semantics=("arbitrary",), vmem_limit_bytes=VMEM_LIMIT_BYTES),
        name="ffn_ln",
    )(x, w1, w3, w2, g, b)


def _ffn_ple_ln_kernel(x_ref, p_ref, w1_ref, w3_ref, w2_ref, g3_ref, b3_ref,
                       wpg_ref, wpp_ref, g4_ref, b4_ref, o_ref):
    x = x_ref[...]
    y = _swiglu(x.astype(BF16), w1_ref, w3_ref, w2_ref)
    x3 = _layer_norm(ALPHA * x + 0.5 * y, g3_ref[...], b3_ref[...])
    gate = jax.nn.sigmoid(jnp.dot(x3.astype(BF16), wpg_ref[...], preferred_element_type=F32))
    emb = jnp.dot(p_ref[...].astype(BF16), wpp_ref[...], preferred_element_type=F32)
    o_ref[...] = _layer_norm(ALPHA * x3 + gate * emb, g4_ref[...], b4_ref[...])


def _ffn_ple_ln(x, p, w1, w3, w2, g3, b3, wpg, wpp, g4, b4, *, tm):
    n, d = x.shape
    f = w1.shape[1]
    pd = p.shape[1]
    return pl.pallas_call(
        _ffn_ple_ln_kernel,
        grid=(n // tm,),
        in_specs=[pl.BlockSpec((tm, d), lambda i: (i, 0)),
                  pl.BlockSpec((tm, pd), lambda i: (i, 0)),
                  _const_spec((d, f)), _const_spec((d, f)), _const_spec((f, d)),
                  _const_spec((1, d)), _const_spec((1, d)),
                  _const_spec((d, d)), _const_spec((pd, d)),
                  _const_spec((1, d)), _const_spec((1, d))],
        out_specs=pl.BlockSpec((tm, d), lambda i: (i, 0)),
        out_shape=jax.ShapeDtypeStruct((n, d), F32),
        compiler_params=pltpu.CompilerParams(
            dimension_semantics=("arbitrary",), vmem_limit_bytes=VMEM_LIMIT_BYTES),
        name="ffn_ple_ln",
    )(x, p, w1, w3, w2, g3, b3, wpg, wpp, g4, b4)


def _proj_mla_kernel(x_ref, win_ref, gq_ref, gkv_ref, wq_ref, wqr_ref, wk_ref, wv_ref,
                     e_ref, cosq_ref, sinq_ref, tk_ref,
                     q_ref, k_ref, vt_ref, u_ref, sa_ref, sb_ref, *, d, ssm_w):
    xb = x_ref[...].astype(BF16)
    proj = jnp.dot(xb, win_ref[...], preferred_element_type=F32)
    o_kv = Q_LORA
    o_u = o_kv + KV_LORA
    o_ga = o_u + ssm_w
    o_gb = o_ga + d
    o_kr = o_gb + d
    c_q = _rms_norm(proj[:, :Q_LORA], gq_ref[...]).astype(BF16)
    c_kv = _rms_norm(proj[:, o_kv:o_u], gkv_ref[...]).astype(BF16)
    u_ref[...] = proj[:, o_u:o_ga].astype(BF16)
    sa_ref[...] = jax.nn.sigmoid(proj[:, o_ga:o_gb]).astype(BF16)
    sb_ref[...] = jax.nn.sigmoid(proj[:, o_gb:o_kr]).astype(BF16)
    kr = (proj[:, o_kr:o_kr + LANES] * tk_ref[...]).astype(BF16)
    q = jnp.dot(c_q, wq_ref[...], preferred_element_type=F32)
    q_rot = jnp.dot(c_q, wqr_ref[...], preferred_element_type=F32)
    k = (jnp.dot(c_kv, wk_ref[...], preferred_element_type=F32)
         + jnp.dot(kr, e_ref[...], preferred_element_type=F32))
    vt = lax.dot_general(wv_ref[...], c_kv, (((1,), (1,)), ((), ())), preferred_element_type=F32)
    cosq = cosq_ref[...]
    sinq = sinq_ref[...]
    rows = lax.broadcasted_iota(jnp.int32, (HEAD_PAD, vt.shape[1]), 0)
    for h in range(MLA_HEADS):
        sl = slice(h * HEAD_PAD, (h + 1) * HEAD_PAD)
        q_ref[0, h] = (q[:, sl] * cosq + q_rot[:, sl] * sinq).astype(BF16)
        k_ref[0, h] = k[:, sl].astype(BF16)
        vt_ref[0, h, 0] = jnp.where(rows == _ones_row(h), 1.0, vt[sl, :]).astype(BF16)


def _proj_mla(x1, win, gq, gkv, wq, wqr, wk, wv, e2, cosq, sinq, tkr, *, batch, seq, tm, ssm_w):
    n, d = x1.shape
    wcols = win.shape[1]
    hp = MLA_HEADS * HEAD_PAD
    spb = seq // tm
    head_spec = pl.BlockSpec((1, MLA_HEADS, tm, HEAD_PAD), lambda i: (i // spb, 0, i % spb, 0))
    head_shape = jax.ShapeDtypeStruct((batch, MLA_HEADS, seq, HEAD_PAD), BF16)
    vt_spec = pl.BlockSpec((1, MLA_HEADS, 1, HEAD_PAD, tm), lambda i: (i // spb, 0, i % spb, 0, 0))
    vt_shape = jax.ShapeDtypeStruct((batch, MLA_HEADS, spb, HEAD_PAD, tm), BF16)
    tab_spec = pl.BlockSpec((tm, LANES), lambda i: (i % spb, 0))
    return pl.pallas_call(
        functools.partial(_proj_mla_kernel, d=d, ssm_w=ssm_w),
        grid=(n // tm,),
        in_specs=[pl.BlockSpec((tm, d), lambda i: (i, 0)),
                  _const_spec((d, wcols)),
                  _const_spec((1, Q_LORA)), _const_spec((1, KV_LORA)),
                  _const_spec((Q_LORA, hp)), _const_spec((Q_LORA, hp)),
                  _const_spec((KV_LORA, hp)), _const_spec((hp, KV_LORA)),
                  _const_spec((LANES, hp)),
                  tab_spec, tab_spec, tab_spec],
        out_specs=[head_spec, head_spec, vt_spec,
                   pl.BlockSpec((tm, ssm_w), lambda i: (i, 0)),
                   pl.BlockSpec((tm, d), lambda i: (i, 0)),
                   pl.BlockSpec((tm, d), lambda i: (i, 0))],
        out_shape=[head_shape, head_shape, vt_shape,
                   jax.ShapeDtypeStruct((n, ssm_w), BF16),
                   jax.ShapeDtypeStruct((n, d), BF16),
                   jax.ShapeDtypeStruct((n, d), BF16)],
        compiler_params=pltpu.CompilerParams(
            dimension_semantics=("arbitrary",), vmem_limit_bytes=VMEM_LIMIT_BYTES),
        name="proj_mla",
    )(x1, win, gq, gkv, wq, wqr, wk, wv, e2, cosq, sinq, tkr)


def _ones_row(h):
    return V_HEAD if h % 2 == 0 else 0


def _flash_kernel(q_ref, k_ref, vt_ref, o_ref, s_sc, p_sc, a_sc, m_sc, acc_sc, *, tk, nk):
    q = q_ref[0, 0]

    def scores(j, slot):
        s_sc[slot] = lax.dot_general(k_ref[0, 0, j * tk:(j + 1) * tk, :], q, (((1,), (1,)), ((), ())),
                                     preferred_element_type=F32)

    def softmax(slot):
        s = s_sc[slot]
        m_prev = m_sc[...]
        m_new = jnp.maximum(m_prev, jnp.max(s, axis=0, keepdims=True))
        a_sc[slot] = jnp.exp(m_prev - m_new)
        p_sc[slot] = jnp.exp(s - m_new).astype(BF16)
        m_sc[...] = m_new

    def weighted_values(j, slot):
        acc_sc[...] = a_sc[slot] * acc_sc[...] + jnp.dot(
            vt_ref[0, 0, j], p_sc[slot], preferred_element_type=F32)

    m_sc[...] = jnp.full(m_sc.shape, -jnp.inf, F32)
    acc_sc[...] = jnp.zeros(acc_sc.shape, F32)
    scores(0, 0)
    for t in range(nk + 1):
        if t + 1 < nk:
            scores(t + 1, (t + 1) % 2)
        if t < nk:
            softmax(t % 2)
        if t >= 1:
            weighted_values(t - 1, (t - 1) % 2)

    odd = pl.program_id(1) % 2
    acc = acc_sc[...]
    l = jnp.where(odd == 1, acc[_ones_row(1):_ones_row(1) + 1], acc[_ones_row(0):_ones_row(0) + 1])
    rows = lax.broadcasted_iota(jnp.int32, acc.shape, 0) - odd * V_HEAD
    is_value = (rows >= 0) & (rows < V_HEAD)
    o_ref[0, 0] = jnp.where(is_value, acc / l, 0.0).T.astype(o_ref.dtype)


def _flash_attn(q, k, vt, *, tq):
    b, h, s, dp = q.shape
    nk, tk = vt.shape[2], vt.shape[4]
    assert nk % 2 == 0 and nk >= 4
    return pl.pallas_call(
        functools.partial(_flash_kernel, tk=tk, nk=nk),
        grid=(b, h, s // tq),
        in_specs=[pl.BlockSpec((1, 1, tq, dp), lambda bi, hi, qi: (bi, hi, qi, 0)),
                  pl.BlockSpec((1, 1, s, dp), lambda bi, hi, qi: (bi, hi, 0, 0)),
                  pl.BlockSpec((1, 1, nk, dp, tk), lambda bi, hi, qi: (bi, hi, 0, 0, 0))],
        out_specs=pl.BlockSpec((1, 1, tq, dp), lambda bi, hi, qi: (bi, hi, qi, 0)),
        out_shape=jax.ShapeDtypeStruct((b, h, s, dp), BF16),
        scratch_shapes=[pltpu.VMEM((2, tk, tq), F32), pltpu.VMEM((2, tk, tq), BF16),
                        pltpu.VMEM((2, 1, tq), F32), pltpu.VMEM((1, tq), F32),
                        pltpu.VMEM((dp, tq), F32)],
        compiler_params=pltpu.CompilerParams(
            dimension_semantics=("arbitrary", "arbitrary", "arbitrary"),
            vmem_limit_bytes=VMEM_LIMIT_BYTES),
        name="flash_attn",
    )(q, k, vt)


def _s5_kernel(u_ref, t_ref, ws_ref, wof_ref, wob_ref, dre_ref, dim_ref, z_ref,
               sre_sc, sim_sc, xfre_sc, xfim_sc, xbre_sc, xbim_sc, *, gb, nk, nb):
    w = 2 * SSM_STATE
    for g in range(gb):
        s = jnp.dot(u_ref[g], ws_ref[g], preferred_element_type=F32)
        sre_sc[:, g * w:(g + 1) * w] = s[:, :w]
        sim_sc[:, g * w:(g + 1) * w] = s[:, w:]

    shape = (SUBLANES, gb * w)
    is_fwd = (lax.broadcasted_iota(jnp.int32, shape, 1) % w) < SSM_STATE
    top = lax.broadcasted_iota(jnp.int32, shape, 0) < nb
    d_re = jnp.broadcast_to(dre_ref[0], shape)
    d_im = jnp.broadcast_to(dim_ref[0], shape)
    swap = lambda a: pltpu.roll(a, nb, 0)

    def step(j, carry):
        re0, im0 = carry
        rf = pl.multiple_of(j * SUBLANES, SUBLANES)
        rb = pl.multiple_of((nk // 2 - 1 - j) * SUBLANES, SUBLANES)
        m_re = jnp.where(is_fwd, sre_sc[pl.ds(rf, SUBLANES), :], swap(sre_sc[pl.ds(rb, SUBLANES), :]))
        m_im = jnp.where(is_fwd, sim_sc[pl.ds(rf, SUBLANES), :], swap(sim_sc[pl.ds(rb, SUBLANES), :]))
        ms_re, ms_im = swap(m_re), swap(m_im)
        re1 = d_re * re0 - d_im * im0 + jnp.where(top, m_re, ms_re)
        im1 = d_re * im0 + d_im * re0 + jnp.where(top, m_im, ms_im)
        re2 = d_re * re1 - d_im * im1 + jnp.where(top, ms_re, m_re)
        im2 = d_re * im1 + d_im * re1 + jnp.where(top, ms_im, m_im)
        xfre_sc[pl.ds(rf, SUBLANES), :] = jnp.where(top, re0, re1)
        xfim_sc[pl.ds(rf, SUBLANES), :] = jnp.where(top, im0, im1)
        xbre_sc[pl.ds(rb, SUBLANES), :] = jnp.where(top, re1, re0)
        xbim_sc[pl.ds(rb, SUBLANES), :] = jnp.where(top, im1, im0)
        return re2, im2

    zero = jnp.zeros(shape, F32)
    lax.fori_loop(0, nk // 2, step, (zero, zero))

    for g in range(gb):
        sl = slice(g * w, (g + 1) * w)
        xf = jnp.concatenate([xfre_sc[:, sl], xfim_sc[:, sl]], axis=1).astype(BF16)
        xb = jnp.concatenate([xbre_sc[:, sl], xbim_sc[:, sl]], axis=1).astype(BF16)
        y = (jnp.dot(u_ref[g], t_ref[g], preferred_element_type=F32)
             + jnp.dot(xf, wof_ref[g], preferred_element_type=F32)
             + jnp.dot(xb, wob_ref[g], preferred_element_type=F32))
        z_ref[g] = jax.nn.gelu(y).astype(z_ref.dtype)


def _s5_scan(u5, t_op, ws, wof, wob, d_re, d_im, *, nk, nb, gb):
    g, r, cw = u5.shape
    w = 2 * SSM_STATE
    grp = lambda i: (i, 0, 0)
    return pl.pallas_call(
        functools.partial(_s5_kernel, gb=gb, nk=nk, nb=nb),
        grid=(g // gb,),
        in_specs=[pl.BlockSpec((gb, r, cw), grp),
                  pl.BlockSpec((gb, cw, cw), grp), pl.BlockSpec((gb, cw, 2 * w), grp),
                  pl.BlockSpec((gb, 2 * w, cw), grp), pl.BlockSpec((gb, 2 * w, cw), grp),
                  pl.BlockSpec((1, 1, gb * w), grp), pl.BlockSpec((1, 1, gb * w), grp)],
        out_specs=pl.BlockSpec((gb, r, cw), grp),
        out_shape=jax.ShapeDtypeStruct((g, r, cw), BF16),
        scratch_shapes=[pltpu.VMEM((r, gb * w), F32) for _ in range(6)],
        compiler_params=pltpu.CompilerParams(
            dimension_semantics=("arbitrary",), vmem_limit_bytes=VMEM_LIMIT_BYTES),
        name="s5_scan",
    )(u5, t_op, ws, wof, wob, d_re, d_im)


def _s5_operators(fwd, bwd, d_skip):
    L = S5_CHUNK
    hi = lax.Precision.HIGHEST

    def disc(lam_re, lam_im, log_dt, b_re, b_im, c_re, c_im):
        lr, li = lam_re.astype(F32), lam_im.astype(F32)
        dt = jnp.exp(log_dt.astype(F32))[:, None]
        steps = jnp.arange(L + 1, dtype=F32)
        mag = jnp.exp((lr * dt)[..., None] * steps)
        ang = (li * dt)[..., None] * steps
        pw_re, pw_im = mag * jnp.cos(ang), mag * jnp.sin(ang)
        xr, xi = pw_re[..., 1] - 1.0, pw_im[..., 1]
        den = lr * lr + li * li
        kr, ki = (xr * lr + xi * li) / den, (xi * lr - xr * li) / den
        br, bi = b_re.astype(F32), b_im.astype(F32)
        bb_re = kr[..., None] * br - ki[..., None] * bi
        bb_im = kr[..., None] * bi + ki[..., None] * br
        e_re = pw_re[..., :L, None] * bb_re[:, :, None, :] - pw_im[..., :L, None] * bb_im[:, :, None, :]
        e_im = pw_re[..., :L, None] * bb_im[:, :, None, :] + pw_im[..., :L, None] * bb_re[:, :, None, :]
        cr, ci = c_re.astype(F32), c_im.astype(F32)
        taps = (jnp.einsum('gcp,gptd->gtcd', cr, e_re, precision=hi)
                - jnp.einsum('gcp,gptd->gtcd', ci, e_im, precision=hi))
        return pw_re, pw_im, e_re, e_im, cr, ci, taps

    pwr_f, pwi_f, er_f, ei_f, cr_f, ci_f, taps_f = disc(*fwd)
    pwr_b, pwi_b, er_b, ei_b, cr_b, ci_b, taps_b = disc(*bwd)
    G, P = pwr_f.shape[:2]
    C = er_f.shape[-1]
    s_idx = jnp.arange(L)[:, None]
    t_idx = jnp.arange(L)[None, :]
    lag = t_idx - s_idx
    tf = jnp.where((lag >= 0)[None, :, :, None, None], taps_f[:, jnp.clip(lag, 0, L - 1)], 0.0)
    tb = jnp.where((lag <= 0)[None, :, :, None, None], taps_b[:, jnp.clip(-lag, 0, L - 1)], 0.0)
    skip = ((lag == 0)[None, :, :, None, None]
            * (jnp.eye(C, dtype=F32)[None, None, None] * d_skip.astype(F32)[:, None, None, :, None]))
    t_op = (tf + tb + skip).transpose(0, 1, 4, 2, 3).reshape(G, L * C, L * C)

    to_sdp = lambda e: e.transpose(0, 2, 3, 1)
    ws = jnp.concatenate(
        [to_sdp(er_f[:, :, ::-1]), to_sdp(er_b), to_sdp(ei_f[:, :, ::-1]), to_sdp(ei_b)],
        axis=-1).reshape(G, L * C, 4 * P)

    def out_map(cr, ci, pr, pi):
        c_r, c_i = cr.transpose(0, 2, 1)[:, :, None, :], ci.transpose(0, 2, 1)[:, :, None, :]
        p_r, p_i = pr[..., None], pi[..., None]
        return ((c_r * p_r - c_i * p_i).reshape(G, P, L * C),
                (c_r * p_i + c_i * p_r).reshape(G, P, L * C))

    of_re, of_im = out_map(cr_f, ci_f, pwr_f[..., 1:L + 1], pwi_f[..., 1:L + 1])
    ob_re, ob_im = out_map(cr_b, ci_b, pwr_b[..., L:0:-1], pwi_b[..., L:0:-1])
    zp = jnp.zeros((G, P, L * C), F32)
    wof = jnp.concatenate([of_re, zp, -of_im, zp], axis=1)
    wob = jnp.concatenate([zp, ob_re, zp, -ob_im], axis=1)

    d_re = jnp.concatenate([pwr_f[..., L], pwr_b[..., L]], axis=-1)
    d_im = jnp.concatenate([pwi_f[..., L], pwi_b[..., L]], axis=-1)
    return (t_op.astype(BF16), ws.astype(BF16), wof.astype(BF16), wob.astype(BF16), d_re, d_im)


def _merge_ln_kernel(x_ref, o_ref, z_ref, sa_ref, sb_ref, woa_ref, wglu_ref, wos_ref, wout_ref,
                     g_ref, b_ref, out_ref):
    oa = jnp.concatenate(
        [o_ref[0, 2 * j] + o_ref[0, 2 * j + 1] for j in range(MLA_HEADS // 2)], axis=1)
    y_a = jnp.dot(oa, woa_ref[...], preferred_element_type=F32)
    z = z_ref[...]
    gl = jax.nn.sigmoid(jnp.dot(z, wglu_ref[...], preferred_element_type=F32))
    y_b = jnp.dot((z.astype(F32) * gl).astype(BF16), wos_ref[...], preferred_element_type=F32)
    merged = (sa_ref[...].astype(F32) * y_a + sb_ref[...].astype(F32) * y_b).astype(BF16)
    r = ALPHA * x_ref[...] + jnp.dot(merged, wout_ref[...], preferred_element_type=F32)
    out_ref[...] = _layer_norm(r, g_ref[...], b_ref[...])


def _merge_ln(x1, o, z, sa, sb, woa, wglu, wos, wout, g, b, *, seq, tm):
    n, d = x1.shape
    sw = z.shape[1]
    spb = seq // tm
    return pl.pallas_call(
        _merge_ln_kernel,
        grid=(n // tm,),
        in_specs=[pl.BlockSpec((tm, d), lambda i: (i, 0)),
                  pl.BlockSpec((1, MLA_HEADS, tm, HEAD_PAD), lambda i: (i // spb, 0, i % spb, 0)),
                  pl.BlockSpec((tm, sw), lambda i: (i, 0)),
                  pl.BlockSpec((tm, d), lambda i: (i, 0)),
                  pl.BlockSpec((tm, d), lambda i: (i, 0)),
                  _const_spec(woa.shape), _const_spec(wglu.shape), _const_spec(wos.shape),
                  _const_spec(wout.shape), _const_spec((1, d)), _const_spec((1, d))],
        out_specs=pl.BlockSpec((tm, d), lambda i: (i, 0)),
        out_shape=jax.ShapeDtypeStruct((n, d), F32),
        compiler_params=pltpu.CompilerParams(
            dimension_semantics=("arbitrary",), vmem_limit_bytes=VMEM_LIMIT_BYTES),
        name="merge_ln",
    )(x1, o, z, sa, sb, woa, wglu, wos, wout, g, b)


def _rot_cols(w):
    half = w.shape[-1] // 2
    return jnp.concatenate([-w[..., half:], w[..., :half]], axis=-1)


def _pad_heads(w, offset):
    r, h, dh = w.shape
    cols = [jnp.zeros((r, HEAD_PAD), w.dtype).at[:, offset(hi):offset(hi) + dh].set(w[:, hi])
            for hi in range(h)]
    return jnp.concatenate(cols, axis=1)


def kernel(x, p, ffn1_w1, ffn1_w3, ffn1_w2, ln1_g, ln1_b, w_in, q_norm_g, kv_norm_g, w_uq, w_uk, w_uv, w_o_attn, ssm_lam_re_f, ssm_lam_im_f, ssm_log_dt_f, ssm_b_re_f, ssm_b_im_f, ssm_c_re_f, ssm_c_im_f, ssm_lam_re_b, ssm_lam_im_b, ssm_log_dt_b, ssm_b_re_b, ssm_b_im_b, ssm_c_re_b, ssm_c_im_b, ssm_d, w_glu, w_o_ssm, w_out, ln2_g, ln2_b, ffn2_w1, ffn2_w3, ffn2_w2, ln3_g, ln3_b, ple_w_proj, ple_w_gate, ln4_g, ln4_b):
    batch, seq, d = x.shape
    n = batch * seq
    ssm_w = w_glu.shape[1]
    groups = ssm_w // SSM_GROUP
    assert DEPTH == 1 and ffn1_w1.shape[0] == 1
    assert seq % (2 * S5_CHUNK) == 0 and groups % S5_GROUP_BLOCK == 0
    assert 2 * batch == SUBLANES, "S5 scan packs two chunks of all batches per sublane tile"
    tm = min(512, seq)
    tq = min(1024, seq)
    i = 0
    row = lambda a: a[i].reshape(1, -1).astype(F32)

    x0 = x.reshape(n, d)
    x1 = _ffn_ln(x0, ffn1_w1[i].astype(BF16), ffn1_w3[i].astype(BF16), ffn1_w2[i].astype(BF16),
                 row(ln1_g), row(ln1_b), tm=tm)

    o_kv = Q_LORA
    o_kr = o_kv + KV_LORA
    o_u = o_kr + QK_ROPE
    o_ga = o_u + ssm_w
    o_gb = o_ga + d
    w = w_in[i]
    w_kr = w[:, o_kr:o_u]
    kr_block = jnp.concatenate(
        [w_kr, _rot_cols(w_kr), jnp.zeros((d, LANES - 2 * QK_ROPE), w.dtype)], axis=1)
    win_ext = jnp.concatenate(
        [w[:, :o_kr], w[:, o_u:o_ga], w[:, o_ga:o_gb], w[:, o_gb:], kr_block], axis=1).astype(BF16)
    wq = w_uq[i]
    wq_rot = jnp.concatenate(
        [jnp.zeros_like(wq[..., :QK_NOPE]), _rot_cols(wq[..., QK_NOPE:])], axis=-1)
    wq_p = _pad_heads(wq, lambda h: 0).astype(BF16)
    wqr_p = _pad_heads(wq_rot, lambda h: 0).astype(BF16)
    wk_p = _pad_heads(w_uk[i], lambda h: 0).astype(BF16)
    wv_pt = _pad_heads(w_uv[i], lambda h: (h % 2) * V_HEAD).T.astype(BF16)
    j = jnp.arange(QK_ROPE)
    e2 = jnp.zeros((LANES, MLA_HEADS, HEAD_PAD), F32)
    e2 = e2.at[j, :, QK_NOPE + j].set(1.0).at[QK_ROPE + j, :, QK_NOPE + j].set(1.0)
    e2 = e2.reshape(LANES, MLA_HEADS * HEAD_PAD).astype(BF16)
    pos = jnp.arange(seq, dtype=F32)
    inv_freq = ROPE_THETA ** (-jnp.arange(0, QK_ROPE, 2, dtype=F32) / QK_ROPE)
    ang = pos[:, None] * inv_freq[None, :]
    cos, sin = jnp.cos(ang), jnp.sin(ang)
    scale = (QK_NOPE + QK_ROPE) ** -0.5
    pad_q = jnp.zeros((seq, HEAD_PAD - QK_NOPE - QK_ROPE), F32)
    cosq = jnp.concatenate([jnp.ones((seq, QK_NOPE), F32), cos, cos, pad_q], axis=1) * scale
    sinq = jnp.concatenate([jnp.zeros((seq, QK_NOPE), F32), sin, sin, pad_q], axis=1) * scale
    tkr = jnp.concatenate([cos, cos, sin, sin, jnp.zeros((seq, LANES - 2 * QK_ROPE), F32)], axis=1)

    q, k, vt, u, sa, sb = _proj_mla(
        x1, win_ext, row(q_norm_g), row(kv_norm_g), wq_p, wqr_p, wk_p, wv_pt, e2, cosq, sinq, tkr,
        batch=batch, seq=seq, tm=tm, ssm_w=ssm_w)

    o = _flash_attn(q, k, vt, tq=tq)

    nk = seq // S5_CHUNK
    fwd = (ssm_lam_re_f[i], ssm_lam_im_f[i], ssm_log_dt_f[i], ssm_b_re_f[i], ssm_b_im_f[i],
           ssm_c_re_f[i], ssm_c_im_f[i])
    bwd = (ssm_lam_re_b[i], ssm_lam_im_b[i], ssm_log_dt_b[i], ssm_b_re_b[i], ssm_b_im_b[i],
           ssm_c_re_b[i], ssm_c_im_b[i])
    t_op, ws, wof, wob, d_re, d_im = _s5_operators(fwd, bwd, ssm_d[i].reshape(groups, SSM_GROUP))
    gb = S5_GROUP_BLOCK
    d_re = d_re.reshape(groups // gb, 1, gb * 2 * SSM_STATE)
    d_im = d_im.reshape(groups // gb, 1, gb * 2 * SSM_STATE)
    u5 = (u.reshape(batch, nk, S5_CHUNK, groups, SSM_GROUP).transpose(3, 1, 0, 2, 4)
          .reshape(groups, nk * batch, S5_CHUNK * SSM_GROUP))
    z5 = _s5_scan(u5, t_op, ws, wof, wob, d_re, d_im, nk=nk, nb=batch, gb=gb)
    z = (z5.reshape(groups, nk, batch, S5_CHUNK, SSM_GROUP).transpose(2, 1, 3, 0, 4)
         .reshape(n, ssm_w))

    x2 = _merge_ln(x1, o, z, sa, sb, w_o_attn[i].astype(BF16), w_glu[i].astype(BF16),
                   w_o_ssm[i].astype(BF16), w_out[i].astype(BF16), row(ln2_g), row(ln2_b),
                   seq=seq, tm=tm)

    x4 = _ffn_ple_ln(x2, p[i].reshape(n, -1), ffn2_w1[i].astype(BF16), ffn2_w3[i].astype(BF16),
                     ffn2_w2[i].astype(BF16), row(ln3_g), row(ln3_b),
                     ple_w_gate[i].astype(BF16), ple_w_proj[i].astype(BF16),
                     row(ln4_g), row(ln4_b), tm=tm)
    return x4.reshape(batch, seq, d)
```

```python
import functools
import math

import jax
import jax.numpy as jnp
from jax import lax
from jax.experimental import pallas as pl
from jax.experimental.pallas import tpu as pltpu

F32 = jnp.float32
BF16 = jnp.bfloat16

DEPTH = 1
MLA_HEADS = 8
QK_NOPE = 64
QK_ROPE = 32
V_HEAD = 64
Q_LORA = 384
KV_LORA = 256
ROPE_THETA = 10000.0
SSM_GROUP = 16
SSM_STATE = 64
LN_EPS = 1e-5
RMS_EPS = 1e-6
ALPHA = (2.0 * DEPTH) ** 0.25

LANES = 128
SUBLANES = 8
VMEM_LIMIT_BYTES = 56 * 1024 * 1024

HEAD_PAD = LANES
S5_CHUNK = 16
S5_GROUP_BLOCK = 2


def _const_spec(shape):
    nd = len(shape)
    return pl.BlockSpec(shape, lambda *_: (0,) * nd, pipeline_mode=pl.Buffered(1))


def _layer_norm(r, g, b):
    mu = jnp.mean(r, axis=-1, keepdims=True)
    c = r - mu
    var = jnp.mean(c * c, axis=-1, keepdims=True)
    return c * lax.rsqrt(var + LN_EPS) * g + b


def _rms_norm(x, g):
    return x * lax.rsqrt(jnp.mean(x * x, axis=-1, keepdims=True) + RMS_EPS) * g


def _swiglu(xb, w1_ref, w3_ref, w2_ref):
    h1 = jnp.dot(xb, w1_ref[...], preferred_element_type=F32)
    h3 = jnp.dot(xb, w3_ref[...], preferred_element_type=F32)
    a = (h1 * jax.nn.sigmoid(h1) * h3).astype(BF16)
    return jnp.dot(a, w2_ref[...], preferred_element_type=F32)


def _ffn_ln_kernel(x_ref, w1_ref, w3_ref, w2_ref, g_ref, b_ref, o_ref):
    x = x_ref[...]
    y = _swiglu(x.astype(BF16), w1_ref, w3_ref, w2_ref)
    o_ref[...] = _layer_norm(ALPHA * x + 0.5 * y, g_ref[...], b_ref[...])


def _ffn_ln(x, w1, w3, w2, g, b, *, tm):
    n, d = x.shape
    f = w1.shape[1]
    return pl.pallas_call(
        _ffn_ln_kernel,
        grid=(n // tm,),
        in_specs=[pl.BlockSpec((tm, d), lambda i: (i, 0)),
                  _const_spec((d, f)), _const_spec((d, f)), _const_spec((f, d)),
                  _const_spec((1, d)), _const_spec((1, d))],
        out_specs=pl.BlockSpec((tm, d), lambda i: (i, 0)),
        out_shape=jax.ShapeDtypeStruct((n, d), F32),
        compiler_params=pltpu.CompilerParams(
            dimension_semantics=("arbitrary",), vmem_limit_bytes=VMEM_LIMIT_BYTES),
        name="ffn_ln",
    )(x, w1, w3, w2, g, b)


def _ffn_ple_ln_kernel(x_ref, p_ref, w1_ref, w3_ref, w2_ref, g3_ref, b3_ref,
                       wpg_ref, wpp_ref, g4_ref, b4_ref, o_ref):
    x = x_ref[...]
    y = _swiglu(x.astype(BF16), w1_ref, w3_ref, w2_ref)
    x3 = _layer_norm(ALPHA * x + 0.5 * y, g3_ref[...], b3_ref[...])
    gate = jax.nn.sigmoid(jnp.dot(x3.astype(BF16), wpg_ref[...], preferred_element_type=F32))
    emb = jnp.dot(p_ref[...].astype(BF16), wpp_ref[...], preferred_element_type=F32)
    o_ref[...] = _layer_norm(ALPHA * x3 + gate * emb, g4_ref[...], b4_ref[...])


def _ffn_ple_ln(x, p, w1, w3, w2, g3, b3, wpg, wpp, g4, b4, *, tm):
    n, d = x.shape
    f = w1.shape[1]
    pd = p.shape[1]
    return pl.pallas_call(
        _ffn_ple_ln_kernel,
        grid=(n // tm,),
        in_specs=[pl.BlockSpec((tm, d), lambda i: (i, 0)),
                  pl.BlockSpec((tm, pd), lambda i: (i, 0)),
                  _const_spec((d, f)), _const_spec((d, f)), _const_spec((f, d)),
                  _const_spec((1, d)), _const_spec((1, d)),
                  _const_spec((d, d)), _const_spec((pd, d)),
                  _const_spec((1, d)), _const_spec((1, d))],
        out_specs=pl.BlockSpec((tm, d), lambda i: (i, 0)),
        out_shape=jax.ShapeDtypeStruct((n, d), F32),
        compiler_params=pltpu.CompilerParams(
            dimension_semantics=("arbitrary",), vmem_limit_bytes=VMEM_LIMIT_BYTES),
        name="ffn_ple_ln",
    )(x, p, w1, w3, w2, g3, b3, wpg, wpp, g4, b4)


GROUPS_PER_TILE = LANES // SSM_GROUP


def _to_chunk_layout(u, out_ref, sc):
    tm = u.shape[0]
    nc = tm // S5_CHUNK
    for j in range(sc.shape[0]):
        sc[j] = u[:, j * LANES:(j + 1) * LANES]
        by_t = [sc[j, pl.ds(t, nc, stride=S5_CHUNK), :] for t in range(S5_CHUNK)]
        for i in range(GROUPS_PER_TILE):
            out_ref[j * GROUPS_PER_TILE + i] = jnp.concatenate(
                [a[:, i * SSM_GROUP:(i + 1) * SSM_GROUP] for a in by_t], axis=1).astype(out_ref.dtype)


def _from_chunk_layout(z_ref, sc):
    nc = z_ref.shape[1]
    for j in range(sc.shape[0]):
        zs = [z_ref[j * GROUPS_PER_TILE + i].astype(F32) for i in range(GROUPS_PER_TILE)]
        for t in range(S5_CHUNK):
            sc[j, pl.ds(t, nc, stride=S5_CHUNK), :] = jnp.concatenate(
                [a[:, t * SSM_GROUP:(t + 1) * SSM_GROUP] for a in zs], axis=1)
    return jnp.concatenate([sc[j] for j in range(sc.shape[0])], axis=1)


def _proj_mla_kernel(x_ref, win_ref, gq_ref, gkv_ref, wq_ref, wqr_ref, wk_ref, wv_ref,
                     e_ref, cosq_ref, sinq_ref, tk_ref,
                     q_ref, k_ref, vt_ref, u_ref, sa_ref, sb_ref, u_sc, *, d, ssm_w):
    xb = x_ref[...].astype(BF16)
    proj = jnp.dot(xb, win_ref[...], preferred_element_type=F32)
    o_kv = Q_LORA
    o_u = o_kv + KV_LORA
    o_ga = o_u + ssm_w
    o_gb = o_ga + d
    o_kr = o_gb + d
    c_q = _rms_norm(proj[:, :Q_LORA], gq_ref[...]).astype(BF16)
    c_kv = _rms_norm(proj[:, o_kv:o_u], gkv_ref[...]).astype(BF16)
    _to_chunk_layout(proj[:, o_u:o_ga], u_ref, u_sc)
    sa_ref[...] = jax.nn.sigmoid(proj[:, o_ga:o_gb]).astype(BF16)
    sb_ref[...] = jax.nn.sigmoid(proj[:, o_gb:o_kr]).astype(BF16)
    kr = (proj[:, o_kr:o_kr + LANES] * tk_ref[...]).astype(BF16)
    q = jnp.dot(c_q, wq_ref[...], preferred_element_type=F32)
    q_rot = jnp.dot(c_q, wqr_ref[...], preferred_element_type=F32)
    k = (jnp.dot(c_kv, wk_ref[...], preferred_element_type=F32)
         + jnp.dot(kr, e_ref[...], preferred_element_type=F32))
    vt = lax.dot_general(wv_ref[...], c_kv, (((1,), (1,)), ((), ())), preferred_element_type=F32)
    cosq = cosq_ref[...]
    sinq = sinq_ref[...]
    rows = lax.broadcasted_iota(jnp.int32, (HEAD_PAD, vt.shape[1]), 0)
    for h in range(MLA_HEADS):
        sl = slice(h * HEAD_PAD, (h + 1) * HEAD_PAD)
        q_ref[0, h] = (q[:, sl] * cosq + q_rot[:, sl] * sinq).astype(BF16)
        k_ref[0, h] = k[:, sl].astype(BF16)
        vt_ref[0, h, 0] = jnp.where(rows == _ones_row(h), 1.0, vt[sl, :]).astype(BF16)


def _proj_mla(x1, win, gq, gkv, wq, wqr, wk, wv, e2, cosq, sinq, tkr, *, batch, seq, tm, ssm_w):
    n, d = x1.shape
    wcols = win.shape[1]
    hp = MLA_HEADS * HEAD_PAD
    spb = seq // tm
    head_spec = pl.BlockSpec((1, MLA_HEADS, tm, HEAD_PAD), lambda i: (i // spb, 0, i % spb, 0))
    head_shape = jax.ShapeDtypeStruct((batch, MLA_HEADS, seq, HEAD_PAD), BF16)
    vt_spec = pl.BlockSpec((1, MLA_HEADS, 1, HEAD_PAD, tm), lambda i: (i // spb, 0, i % spb, 0, 0))
    vt_shape = jax.ShapeDtypeStruct((batch, MLA_HEADS, spb, HEAD_PAD, tm), BF16)
    tab_spec = pl.BlockSpec((tm, LANES), lambda i: (i % spb, 0))
    groups, chunk_w = ssm_w // SSM_GROUP, S5_CHUNK * SSM_GROUP
    return pl.pallas_call(
        functools.partial(_proj_mla_kernel, d=d, ssm_w=ssm_w),
        grid=(n // tm,),
        in_specs=[pl.BlockSpec((tm, d), lambda i: (i, 0)),
                  _const_spec((d, wcols)),
                  _const_spec((1, Q_LORA)), _const_spec((1, KV_LORA)),
                  _const_spec((Q_LORA, hp)), _const_spec((Q_LORA, hp)),
                  _const_spec((KV_LORA, hp)), _const_spec((hp, KV_LORA)),
                  _const_spec((LANES, hp)),
                  tab_spec, tab_spec, tab_spec],
        out_specs=[head_spec, head_spec, vt_spec,
                   pl.BlockSpec((groups, tm // S5_CHUNK, chunk_w), lambda i: (0, i, 0)),
                   pl.BlockSpec((tm, d), lambda i: (i, 0)),
                   pl.BlockSpec((tm, d), lambda i: (i, 0))],
        out_shape=[head_shape, head_shape, vt_shape,
                   jax.ShapeDtypeStruct((groups, n // S5_CHUNK, chunk_w), BF16),
                   jax.ShapeDtypeStruct((n, d), BF16),
                   jax.ShapeDtypeStruct((n, d), BF16)],
        scratch_shapes=[pltpu.VMEM((ssm_w // LANES, tm, LANES), F32)],
        compiler_params=pltpu.CompilerParams(
            dimension_semantics=("arbitrary",), vmem_limit_bytes=VMEM_LIMIT_BYTES),
        name="proj_mla",
    )(x1, win, gq, gkv, wq, wqr, wk, wv, e2, cosq, sinq, tkr)


def _ones_row(h):
    return V_HEAD if h % 2 == 0 else 0


def _flash_kernel(q_ref, k_ref, vt_ref, o_ref, s_sc, p_sc, a_sc, m_sc, acc_sc, *, tk, nk):
    q = q_ref[0, 0]

    def scores(j, slot):
        s_sc[slot] = lax.dot_general(k_ref[0, 0, j * tk:(j + 1) * tk, :], q, (((1,), (1,)), ((), ())),
                                     preferred_element_type=F32)

    def softmax(slot):
        s = s_sc[slot]
        m_prev = m_sc[...]
        m_new = jnp.maximum(m_prev, jnp.max(s, axis=0, keepdims=True))
        a_sc[slot] = jnp.exp(m_prev - m_new)
        p_sc[slot] = jnp.exp(s - m_new).astype(BF16)
        m_sc[...] = m_new

    def weighted_values(j, slot):
        acc_sc[...] = a_sc[slot] * acc_sc[...] + jnp.dot(
            vt_ref[0, 0, j], p_sc[slot], preferred_element_type=F32)

    m_sc[...] = jnp.full(m_sc.shape, -jnp.inf, F32)
    acc_sc[...] = jnp.zeros(acc_sc.shape, F32)
    scores(0, 0)
    for t in range(nk + 1):
        if t + 1 < nk:
            scores(t + 1, (t + 1) % 2)
        if t < nk:
            softmax(t % 2)
        if t >= 1:
            weighted_values(t - 1, (t - 1) % 2)

    odd = pl.program_id(1) % 2
    acc = acc_sc[...]
    l = jnp.where(odd == 1, acc[_ones_row(1):_ones_row(1) + 1], acc[_ones_row(0):_ones_row(0) + 1])
    rows = lax.broadcasted_iota(jnp.int32, acc.shape, 0) - odd * V_HEAD
    is_value = (rows >= 0) & (rows < V_HEAD)
    o_ref[0, 0] = jnp.where(is_value, acc / l, 0.0).T.astype(o_ref.dtype)


def _flash_attn(q, k, vt, *, tq):
    b, h, s, dp = q.shape
    nk, tk = vt.shape[2], vt.shape[4]
    assert nk % 2 == 0 and nk >= 4
    return pl.pallas_call(
        functools.partial(_flash_kernel, tk=tk, nk=nk),
        grid=(b, h, s // tq),
        in_specs=[pl.BlockSpec((1, 1, tq, dp), lambda bi, hi, qi: (bi, hi, qi, 0)),
                  pl.BlockSpec((1, 1, s, dp), lambda bi, hi, qi: (bi, hi, 0, 0)),
                  pl.BlockSpec((1, 1, nk, dp, tk), lambda bi, hi, qi: (bi, hi, 0, 0, 0))],
        out_specs=pl.BlockSpec((1, 1, tq, dp), lambda bi, hi, qi: (bi, hi, qi, 0)),
        out_shape=jax.ShapeDtypeStruct((b, h, s, dp), BF16),
        scratch_shapes=[pltpu.VMEM((2, tk, tq), F32), pltpu.VMEM((2, tk, tq), BF16),
                        pltpu.VMEM((2, 1, tq), F32), pltpu.VMEM((1, tq), F32),
                        pltpu.VMEM((dp, tq), F32)],
        compiler_params=pltpu.CompilerParams(
            dimension_semantics=("arbitrary", "arbitrary", "arbitrary"),
            vmem_limit_bytes=VMEM_LIMIT_BYTES),
        name="flash_attn",
    )(q, k, vt)


def _seg_shift(a, step, pos, seg, is_fwd):
    n = a.shape[0]
    down = jnp.where(pos >= step, pltpu.roll(a, step, 0), 0.0)
    up = jnp.where(pos < seg - step, pltpu.roll(a, n - step, 0), 0.0)
    return jnp.where(is_fwd, down, up)


def _decayed_scan(re, im, dec_ref, first_pow, seg, is_fwd):
    pos = lax.broadcasted_iota(jnp.int32, re.shape, 0) % seg
    step, i = 1, first_pow
    while step < seg:
        dr, di = dec_ref[0, i:i + 1, :], dec_ref[1, i:i + 1, :]
        sr = _seg_shift(re, step, pos, seg, is_fwd)
        si = _seg_shift(im, step, pos, seg, is_fwd)
        re, im = re + dr * sr - di * si, im + dr * si + di * sr
        step, i = 2 * step, i + 1
    return re, im


def _s5_kernel(u_ref, t_ref, ws_ref, wo_ref, dec_ref, pow_ref, exp_ref, z_ref, l_sc, *, gb, nk):
    w = 2 * SSM_STATE
    r = u_ref.shape[1]
    tiles = r // SUBLANES
    planes = [jnp.dot(u_ref[g], ws_ref[g], preferred_element_type=F32) for g in range(gb)]
    s_re = jnp.concatenate([a[:, :w] for a in planes], axis=1)
    s_im = jnp.concatenate([a[:, w:] for a in planes], axis=1)
    fwd = lambda rows: (lax.broadcasted_iota(jnp.int32, (rows, gb * w), 1) % w) < SSM_STATE

    l_re, l_im = _decayed_scan(s_re, s_im, dec_ref, 0, SUBLANES, fwd(r))

    for j in range(gb):
        l_sc[0, j] = l_re[:, j * w:(j + 1) * w]
        l_sc[1, j] = l_im[:, j * w:(j + 1) * w]
    fwd_w = lax.broadcasted_iota(jnp.int32, (tiles, w), 1) < SSM_STATE
    ends = [jnp.concatenate(
        [jnp.where(fwd_w, l_sc[c, j, pl.ds(SUBLANES - 1, tiles, stride=SUBLANES), :],
                   l_sc[c, j, pl.ds(0, tiles, stride=SUBLANES), :]) for j in range(gb)], axis=1)
        for c in range(2)]
    seg2 = nk // SUBLANES
    pos2 = lax.broadcasted_iota(jnp.int32, ends[0].shape, 0) % seg2
    e_re = _seg_shift(ends[0], 1, pos2, seg2, fwd(tiles))
    e_im = _seg_shift(ends[1], 1, pos2, seg2, fwd(tiles))
    c_re, c_im = _decayed_scan(e_re, e_im, dec_ref, 3, seg2, fwd(tiles))

    z_re = jnp.dot(exp_ref[...], c_re.astype(BF16), preferred_element_type=F32)
    z_im = jnp.dot(exp_ref[...], c_im.astype(BF16), preferred_element_type=F32)
    p_re = jnp.broadcast_to(pow_ref[0][None], (tiles, SUBLANES, gb * w)).reshape(r, gb * w)
    p_im = jnp.broadcast_to(pow_ref[1][None], (tiles, SUBLANES, gb * w)).reshape(r, gb * w)
    pos1 = lax.broadcasted_iota(jnp.int32, (r, gb * w), 0) % SUBLANES
    x_re = _seg_shift(l_re, 1, pos1, SUBLANES, fwd(r)) + p_re * z_re - p_im * z_im
    x_im = _seg_shift(l_im, 1, pos1, SUBLANES, fwd(r)) + p_re * z_im + p_im * z_re

    for g in range(gb):
        sl = slice(g * w, (g + 1) * w)
        xin = jnp.concatenate([x_re[:, sl], x_im[:, sl]], axis=1).astype(BF16)
        y = (jnp.dot(u_ref[g], t_ref[g], preferred_element_type=F32)
             + jnp.dot(xin, wo_ref[g], preferred_element_type=F32))
        z_ref[g] = jax.nn.gelu(y).astype(z_ref.dtype)


def _s5_scan(u5, t_op, ws, wo, dec, pows, expand, *, nk, gb):
    g, r, cw = u5.shape
    w = 2 * SSM_STATE
    grp = lambda i: (i, 0, 0)
    par = lambda a: pl.BlockSpec((None,) + a.shape[1:], lambda i: (i, 0, 0, 0))
    return pl.pallas_call(
        functools.partial(_s5_kernel, gb=gb, nk=nk),
        grid=(g // gb,),
        in_specs=[pl.BlockSpec((gb, r, cw), grp),
                  pl.BlockSpec((gb, cw, cw), grp), pl.BlockSpec((gb, cw, 2 * w), grp),
                  pl.BlockSpec((gb, 2 * w, cw), grp),
                  par(dec), par(pows), _const_spec(expand.shape)],
        out_specs=pl.BlockSpec((gb, r, cw), grp),
        out_shape=jax.ShapeDtypeStruct((g, r, cw), BF16),
        scratch_shapes=[pltpu.VMEM((2, gb, r, w), F32)],
        compiler_params=pltpu.CompilerParams(
            dimension_semantics=("arbitrary",), vmem_limit_bytes=VMEM_LIMIT_BYTES),
        name="s5_scan",
    )(u5, t_op, ws, wo, dec, pows, expand)


def _s5_operators(fwd, bwd, d_skip, n_doublings):
    L = S5_CHUNK
    hi = lax.Precision.HIGHEST

    def disc(lam_re, lam_im, log_dt, b_re, b_im, c_re, c_im):
        lr, li = lam_re.astype(F32), lam_im.astype(F32)
        dt = jnp.exp(log_dt.astype(F32))[:, None]
        steps = jnp.arange(L + 1, dtype=F32)
        mag = jnp.exp((lr * dt)[..., None] * steps)
        ang = (li * dt)[..., None] * steps
        pw_re, pw_im = mag * jnp.cos(ang), mag * jnp.sin(ang)
        xr, xi = pw_re[..., 1] - 1.0, pw_im[..., 1]
        den = lr * lr + li * li
        kr, ki = (xr * lr + xi * li) / den, (xi * lr - xr * li) / den
        br, bi = b_re.astype(F32), b_im.astype(F32)
        bb_re = kr[..., None] * br - ki[..., None] * bi
        bb_im = kr[..., None] * bi + ki[..., None] * br
        e_re = pw_re[..., :L, None] * bb_re[:, :, None, :] - pw_im[..., :L, None] * bb_im[:, :, None, :]
        e_im = pw_re[..., :L, None] * bb_im[:, :, None, :] + pw_im[..., :L, None] * bb_re[:, :, None, :]
        cr, ci = c_re.astype(F32), c_im.astype(F32)
        taps = (jnp.einsum('gcp,gptd->gtcd', cr, e_re, precision=hi)
                - jnp.einsum('gcp,gptd->gtcd', ci, e_im, precision=hi))
        return pw_re, pw_im, e_re, e_im, cr, ci, taps

    pwr_f, pwi_f, er_f, ei_f, cr_f, ci_f, taps_f = disc(*fwd)
    pwr_b, pwi_b, er_b, ei_b, cr_b, ci_b, taps_b = disc(*bwd)
    G, P = pwr_f.shape[:2]
    C = er_f.shape[-1]
    s_idx = jnp.arange(L)[:, None]
    t_idx = jnp.arange(L)[None, :]
    lag = t_idx - s_idx
    tf = jnp.where((lag >= 0)[None, :, :, None, None], taps_f[:, jnp.clip(lag, 0, L - 1)], 0.0)
    tb = jnp.where((lag <= 0)[None, :, :, None, None], taps_b[:, jnp.clip(-lag, 0, L - 1)], 0.0)
    skip = ((lag == 0)[None, :, :, None, None]
            * (jnp.eye(C, dtype=F32)[None, None, None] * d_skip.astype(F32)[:, None, None, :, None]))
    t_op = (tf + tb + skip).transpose(0, 1, 4, 2, 3).reshape(G, L * C, L * C)

    to_sdp = lambda e: e.transpose(0, 2, 3, 1)
    ws = jnp.concatenate(
        [to_sdp(er_f[:, :, ::-1]), to_sdp(er_b), to_sdp(ei_f[:, :, ::-1]), to_sdp(ei_b)],
        axis=-1).reshape(G, L * C, 4 * P)

    def out_map(cr, ci, pr, pi):
        c_r, c_i = cr.transpose(0, 2, 1)[:, :, None, :], ci.transpose(0, 2, 1)[:, :, None, :]
        p_r, p_i = pr[..., None], pi[..., None]
        return ((c_r * p_r - c_i * p_i).reshape(G, P, L * C),
                (c_r * p_i + c_i * p_r).reshape(G, P, L * C))

    of_re, of_im = out_map(cr_f, ci_f, pwr_f[..., 1:L + 1], pwi_f[..., 1:L + 1])
    ob_re, ob_im = out_map(cr_b, ci_b, pwr_b[..., L:0:-1], pwi_b[..., L:0:-1])
    wo = jnp.concatenate([of_re, ob_re, -of_im, -ob_im], axis=1)

    def chunk_pow(lam_re, lam_im, log_dt, mult):
        dt = jnp.exp(log_dt.astype(F32))[:, None]
        mag = jnp.exp((lam_re.astype(F32) * dt)[:, None, :] * (L * mult)[None, :, None])
        ang = (lam_im.astype(F32) * dt)[:, None, :] * (L * mult)[None, :, None]
        return mag * jnp.cos(ang), mag * jnp.sin(ang)

    doubling = 2.0 ** jnp.arange(n_doublings, dtype=F32)
    in_tile = jnp.arange(SUBLANES, dtype=F32)
    dec = [jnp.concatenate([f, b], axis=-1) for f, b in
           zip(chunk_pow(*fwd[:3], doubling), chunk_pow(*bwd[:3], doubling))]
    pows = [jnp.concatenate([f, b], axis=-1) for f, b in
            zip(chunk_pow(*fwd[:3], in_tile), chunk_pow(*bwd[:3], in_tile[::-1]))]
    return (t_op.astype(BF16), ws.astype(BF16), wo.astype(BF16),
            jnp.stack(dec, axis=1), jnp.stack(pows, axis=1))


def _merge_ln_kernel(x_ref, o_ref, z_ref, sa_ref, sb_ref, woa_ref, wglu_ref, wos_ref, wout_ref,
                     g_ref, b_ref, out_ref, z_sc):
    oa = jnp.concatenate(
        [o_ref[0, 2 * j] + o_ref[0, 2 * j + 1] for j in range(MLA_HEADS // 2)], axis=1)
    y_a = jnp.dot(oa, woa_ref[...], preferred_element_type=F32)
    z = _from_chunk_layout(z_ref, z_sc)
    gl = jax.nn.sigmoid(jnp.dot(z.astype(BF16), wglu_ref[...], preferred_element_type=F32))
    y_b = jnp.dot((z * gl).astype(BF16), wos_ref[...], preferred_element_type=F32)
    merged = (sa_ref[...].astype(F32) * y_a + sb_ref[...].astype(F32) * y_b).astype(BF16)
    r = ALPHA * x_ref[...] + jnp.dot(merged, wout_ref[...], preferred_element_type=F32)
    out_ref[...] = _layer_norm(r, g_ref[...], b_ref[...])


def _merge_ln(x1, o, z, sa, sb, woa, wglu, wos, wout, g, b, *, seq, tm):
    n, d = x1.shape
    groups, _, chunk_w = z.shape
    spb = seq // tm
    return pl.pallas_call(
        _merge_ln_kernel,
        grid=(n // tm,),
        in_specs=[pl.BlockSpec((tm, d), lambda i: (i, 0)),
                  pl.BlockSpec((1, MLA_HEADS, tm, HEAD_PAD), lambda i: (i // spb, 0, i % spb, 0)),
                  pl.BlockSpec((groups, tm // S5_CHUNK, chunk_w), lambda i: (0, i, 0)),
                  pl.BlockSpec((tm, d), lambda i: (i, 0)),
                  pl.BlockSpec((tm, d), lambda i: (i, 0)),
                  _const_spec(woa.shape), _const_spec(wglu.shape), _const_spec(wos.shape),
                  _const_spec(wout.shape), _const_spec((1, d)), _const_spec((1, d))],
        out_specs=pl.BlockSpec((tm, d), lambda i: (i, 0)),
        out_shape=jax.ShapeDtypeStruct((n, d), F32),
        scratch_shapes=[pltpu.VMEM((groups * SSM_GROUP // LANES, tm, LANES), F32)],
        compiler_params=pltpu.CompilerParams(
            dimension_semantics=("arbitrary",), vmem_limit_bytes=VMEM_LIMIT_BYTES),
        name="merge_ln",
    )(x1, o, z, sa, sb, woa, wglu, wos, wout, g, b)


def _rot_cols(w):
    half = w.shape[-1] // 2
    return jnp.concatenate([-w[..., half:], w[..., :half]], axis=-1)


def _pad_heads(w, offset):
    r, h, dh = w.shape
    cols = [jnp.zeros((r, HEAD_PAD), w.dtype).at[:, offset(hi):offset(hi) + dh].set(w[:, hi])
            for hi in range(h)]
    return jnp.concatenate(cols, axis=1)


def kernel(x, p, ffn1_w1, ffn1_w3, ffn1_w2, ln1_g, ln1_b, w_in, q_norm_g, kv_norm_g, w_uq, w_uk, w_uv, w_o_attn, ssm_lam_re_f, ssm_lam_im_f, ssm_log_dt_f, ssm_b_re_f, ssm_b_im_f, ssm_c_re_f, ssm_c_im_f, ssm_lam_re_b, ssm_lam_im_b, ssm_log_dt_b, ssm_b_re_b, ssm_b_im_b, ssm_c_re_b, ssm_c_im_b, ssm_d, w_glu, w_o_ssm, w_out, ln2_g, ln2_b, ffn2_w1, ffn2_w3, ffn2_w2, ln3_g, ln3_b, ple_w_proj, ple_w_gate, ln4_g, ln4_b):
    batch, seq, d = x.shape
    n = batch * seq
    ssm_w = w_glu.shape[1]
    groups = ssm_w // SSM_GROUP
    assert DEPTH == 1 and ffn1_w1.shape[0] == 1
    assert seq % (SUBLANES * S5_CHUNK) == 0 and groups % S5_GROUP_BLOCK == 0
    assert ssm_w % LANES == 0
    tm = min(512, seq)
    tq = min(1024, seq)
    i = 0
    row = lambda a: a[i].reshape(1, -1).astype(F32)

    x0 = x.reshape(n, d)
    x1 = _ffn_ln(x0, ffn1_w1[i].astype(BF16), ffn1_w3[i].astype(BF16), ffn1_w2[i].astype(BF16),
                 row(ln1_g), row(ln1_b), tm=tm)

    o_kv = Q_LORA
    o_kr = o_kv + KV_LORA
    o_u = o_kr + QK_ROPE
    o_ga = o_u + ssm_w
    o_gb = o_ga + d
    w = w_in[i]
    w_kr = w[:, o_kr:o_u]
    kr_block = jnp.concatenate(
        [w_kr, _rot_cols(w_kr), jnp.zeros((d, LANES - 2 * QK_ROPE), w.dtype)], axis=1)
    win_ext = jnp.concatenate(
        [w[:, :o_kr], w[:, o_u:o_ga], w[:, o_ga:o_gb], w[:, o_gb:], kr_block], axis=1).astype(BF16)
    wq = w_uq[i]
    wq_rot = jnp.concatenate(
        [jnp.zeros_like(wq[..., :QK_NOPE]), _rot_cols(wq[..., QK_NOPE:])], axis=-1)
    wq_p = _pad_heads(wq, lambda h: 0).astype(BF16)
    wqr_p = _pad_heads(wq_rot, lambda h: 0).astype(BF16)
    wk_p = _pad_heads(w_uk[i], lambda h: 0).astype(BF16)
    wv_pt = _pad_heads(w_uv[i], lambda h: (h % 2) * V_HEAD).T.astype(BF16)
    j = jnp.arange(QK_ROPE)
    e2 = jnp.zeros((LANES, MLA_HEADS, HEAD_PAD), F32)
    e2 = e2.at[j, :, QK_NOPE + j].set(1.0).at[QK_ROPE + j, :, QK_NOPE + j].set(1.0)
    e2 = e2.reshape(LANES, MLA_HEADS * HEAD_PAD).astype(BF16)
    pos = jnp.arange(seq, dtype=F32)
    inv_freq = ROPE_THETA ** (-jnp.arange(0, QK_ROPE, 2, dtype=F32) / QK_ROPE)
    ang = pos[:, None] * inv_freq[None, :]
    cos, sin = jnp.cos(ang), jnp.sin(ang)
    scale = (QK_NOPE + QK_ROPE) ** -0.5
    pad_q = jnp.zeros((seq, HEAD_PAD - QK_NOPE - QK_ROPE), F32)
    cosq = jnp.concatenate([jnp.ones((seq, QK_NOPE), F32), cos, cos, pad_q], axis=1) * scale
    sinq = jnp.concatenate([jnp.zeros((seq, QK_NOPE), F32), sin, sin, pad_q], axis=1) * scale
    tkr = jnp.concatenate([cos, cos, sin, sin, jnp.zeros((seq, LANES - 2 * QK_ROPE), F32)], axis=1)

    q, k, vt, u5, sa, sb = _proj_mla(
        x1, win_ext, row(q_norm_g), row(kv_norm_g), wq_p, wqr_p, wk_p, wv_pt, e2, cosq, sinq, tkr,
        batch=batch, seq=seq, tm=tm, ssm_w=ssm_w)

    o = _flash_attn(q, k, vt, tq=tq)

    nk = seq // S5_CHUNK
    fwd = (ssm_lam_re_f[i], ssm_lam_im_f[i], ssm_log_dt_f[i], ssm_b_re_f[i], ssm_b_im_f[i],
           ssm_c_re_f[i], ssm_c_im_f[i])
    bwd = (ssm_lam_re_b[i], ssm_lam_im_b[i], ssm_log_dt_b[i], ssm_b_re_b[i], ssm_b_im_b[i],
           ssm_c_re_b[i], ssm_c_im_b[i])
    tiles_per_seq = nk // SUBLANES
    n_doublings = 3 + max(0, (tiles_per_seq - 1).bit_length())
    t_op, ws, wo, dec, pows = _s5_operators(
        fwd, bwd, ssm_d[i].reshape(groups, SSM_GROUP), n_doublings)
    gb = S5_GROUP_BLOCK
    per_block = lambda a: (a.reshape(groups // gb, gb, *a.shape[1:]).transpose(0, 2, 3, 1, 4)
                           .reshape(groups // gb, a.shape[1], a.shape[2], gb * a.shape[3]))
    rows = n // S5_CHUNK
    expand = (jnp.arange(rows)[:, None] // SUBLANES == jnp.arange(rows // SUBLANES)[None, :]).astype(BF16)
    z5 = _s5_scan(u5, t_op, ws, wo, per_block(dec), per_block(pows), expand, nk=nk, gb=gb)

    x2 = _merge_ln(x1, o, z5, sa, sb, w_o_attn[i].astype(BF16), w_glu[i].astype(BF16),
                   w_o_ssm[i].astype(BF16), w_out[i].astype(BF16), row(ln2_g), row(ln2_b),
                   seq=seq, tm=tm)

    x4 = _ffn_ple_ln(x2, p[i].reshape(n, -1), ffn2_w1[i].astype(BF16), ffn2_w3[i].astype(BF16),
                     ffn2_w2[i].astype(BF16), row(ln3_g), row(ln3_b),
                     ple_w_gate[i].astype(BF16), ple_w_proj[i].astype(BF16),
                     row(ln4_g), row(ln4_b), tm=tm)
    return x4.reshape(batch, seq, d)
```

```python
import functools
import math

import jax
import jax.numpy as jnp
from jax import lax
from jax.experimental import pallas as pl
from jax.experimental.pallas import tpu as pltpu

F32 = jnp.float32
BF16 = jnp.bfloat16

DEPTH = 1
MLA_HEADS = 8
QK_NOPE = 64
QK_ROPE = 32
V_HEAD = 64
Q_LORA = 384
KV_LORA = 256
ROPE_THETA = 10000.0
SSM_GROUP = 16
SSM_STATE = 64
LN_EPS = 1e-5
RMS_EPS = 1e-6
ALPHA = (2.0 * DEPTH) ** 0.25

LANES = 128
SUBLANES = 8
VMEM_LIMIT_BYTES = 56 * 1024 * 1024

HEAD_PAD = LANES
S5_CHUNK = 16
S5_GROUP_BLOCK = 2
FLASH_BUFFERS = 2


def _const_spec(shape):
    nd = len(shape)
    return pl.BlockSpec(shape, lambda *_: (0,) * nd, pipeline_mode=pl.Buffered(1))


def _layer_norm(r, g, b):
    mu = jnp.mean(r, axis=-1, keepdims=True)
    c = r - mu
    var = jnp.mean(c * c, axis=-1, keepdims=True)
    return c * lax.rsqrt(var + LN_EPS) * g + b


def _rms_norm(x, g):
    return x * lax.rsqrt(jnp.mean(x * x, axis=-1, keepdims=True) + RMS_EPS) * g


def _swiglu(xb, w1_ref, w3_ref, w2_ref):
    h1 = jnp.dot(xb, w1_ref[...], preferred_element_type=F32)
    h3 = jnp.dot(xb, w3_ref[...], preferred_element_type=F32)
    a = (h1 * jax.nn.sigmoid(h1) * h3).astype(BF16)
    return jnp.dot(a, w2_ref[...], preferred_element_type=F32)


def _ffn_ln_kernel(x_ref, w1_ref, w3_ref, w2_ref, g_ref, b_ref, o_ref):
    x = x_ref[...]
    y = _swiglu(x.astype(BF16), w1_ref, w3_ref, w2_ref)
    o_ref[...] = _layer_norm(ALPHA * x + 0.5 * y, g_ref[...], b_ref[...])


def _ffn_ln(x, w1, w3, w2, g, b, *, tm):
    n, d = x.shape
    f = w1.shape[1]
    return pl.pallas_call(
        _ffn_ln_kernel,
        grid=(n // tm,),
        in_specs=[pl.BlockSpec((tm, d), lambda i: (i, 0)),
                  _const_spec((d, f)), _const_spec((d, f)), _const_spec((f, d)),
                  _const_spec((1, d)), _const_spec((1, d))],
        out_specs=pl.BlockSpec((tm, d), lambda i: (i, 0)),
        out_shape=jax.ShapeDtypeStruct((n, d), F32),
        compiler_params=pltpu.CompilerParams(
            dimension_semantics=("arbitrary",), vmem_limit_bytes=VMEM_LIMIT_BYTES),
        name="ffn_ln",
    )(x, w1, w3, w2, g, b)


def _ffn_ple_ln_kernel(x_ref, p_ref, w1_ref, w3_ref, w2_ref, g3_ref, b3_ref,
                       wpg_ref, wpp_ref, g4_ref, b4_ref, o_ref):
    x = x_ref[...]
    y = _swiglu(x.astype(BF16), w1_ref, w3_ref, w2_ref)
    x3 = _layer_norm(ALPHA * x + 0.5 * y, g3_ref[...], b3_ref[...])
    gate = jax.nn.sigmoid(jnp.dot(x3.astype(BF16), wpg_ref[...], preferred_element_type=F32))
    emb = jnp.dot(p_ref[...].astype(BF16), wpp_ref[...], preferred_element_type=F32)
    o_ref[...] = _layer_norm(ALPHA * x3 + gate * emb, g4_ref[...], b4_ref[...])


def _ffn_ple_ln(x, p, w1, w3, w2, g3, b3, wpg, wpp, g4, b4, *, tm):
    n, d = x.shape
    f = w1.shape[1]
    pd = p.shape[1]
    return pl.pallas_call(
        _ffn_ple_ln_kernel,
        grid=(n // tm,),
        in_specs=[pl.BlockSpec((tm, d), lambda i: (i, 0)),
                  pl.BlockSpec((tm, pd), lambda i: (i, 0)),
                  _const_spec((d, f)), _const_spec((d, f)), _const_spec((f, d)),
                  _const_spec((1, d)), _const_spec((1, d)),
                  _const_spec((d, d)), _const_spec((pd, d)),
                  _const_spec((1, d)), _const_spec((1, d))],
        out_specs=pl.BlockSpec((tm, d), lambda i: (i, 0)),
        out_shape=jax.ShapeDtypeStruct((n, d), F32),
        compiler_params=pltpu.CompilerParams(
            dimension_semantics=("arbitrary",), vmem_limit_bytes=VMEM_LIMIT_BYTES),
        name="ffn_ple_ln",
    )(x, p, w1, w3, w2, g3, b3, wpg, wpp, g4, b4)


GROUPS_PER_TILE = LANES // SSM_GROUP


def _to_chunk_layout(u, out_ref, sc):
    tm = u.shape[0]
    nc = tm // S5_CHUNK
    for j in range(sc.shape[0]):
        sc[j] = u[:, j * LANES:(j + 1) * LANES]
        by_t = [sc[j, pl.ds(t, nc, stride=S5_CHUNK), :] for t in range(S5_CHUNK)]
        for i in range(GROUPS_PER_TILE):
            out_ref[j * GROUPS_PER_TILE + i] = jnp.concatenate(
                [a[:, i * SSM_GROUP:(i + 1) * SSM_GROUP] for a in by_t], axis=1).astype(out_ref.dtype)


def _from_chunk_layout(z_ref, sc):
    nc = z_ref.shape[1]
    for j in range(sc.shape[0]):
        zs = [z_ref[j * GROUPS_PER_TILE + i].astype(F32) for i in range(GROUPS_PER_TILE)]
        for t in range(S5_CHUNK):
            sc[j, pl.ds(t, nc, stride=S5_CHUNK), :] = jnp.concatenate(
                [a[:, t * SSM_GROUP:(t + 1) * SSM_GROUP] for a in zs], axis=1)
    return jnp.concatenate([sc[j] for j in range(sc.shape[0])], axis=1)


def _proj_mla_kernel(x_ref, win_ref, gq_ref, gkv_ref, wq_ref, wqr_ref, wk_ref, wv_ref,
                     e_ref, cosq_ref, sinq_ref, tk_ref,
                     q_ref, k_ref, vt_ref, u_ref, sa_ref, sb_ref, u_sc, *, d, ssm_w):
    xb = x_ref[...].astype(BF16)
    proj = jnp.dot(xb, win_ref[...], preferred_element_type=F32)
    o_kv = Q_LORA
    o_u = o_kv + KV_LORA
    o_ga = o_u + ssm_w
    o_gb = o_ga + d
    o_kr = o_gb + d
    c_q = _rms_norm(proj[:, :Q_LORA], gq_ref[...]).astype(BF16)
    c_kv = _rms_norm(proj[:, o_kv:o_u], gkv_ref[...]).astype(BF16)
    _to_chunk_layout(proj[:, o_u:o_ga], u_ref, u_sc)
    sa_ref[...] = jax.nn.sigmoid(proj[:, o_ga:o_gb]).astype(BF16)
    sb_ref[...] = jax.nn.sigmoid(proj[:, o_gb:o_kr]).astype(BF16)
    kr = (proj[:, o_kr:o_kr + LANES] * tk_ref[...]).astype(BF16)
    q = jnp.dot(c_q, wq_ref[...], preferred_element_type=F32)
    q_rot = jnp.dot(c_q, wqr_ref[...], preferred_element_type=F32)
    k = (jnp.dot(c_kv, wk_ref[...], preferred_element_type=F32)
         + jnp.dot(kr, e_ref[...], preferred_element_type=F32))
    vt = lax.dot_general(wv_ref[...], c_kv, (((1,), (1,)), ((), ())), preferred_element_type=F32)
    cosq = cosq_ref[...]
    sinq = sinq_ref[...]
    rows = lax.broadcasted_iota(jnp.int32, (HEAD_PAD, vt.shape[1]), 0)
    for h in range(MLA_HEADS):
        sl = slice(h * HEAD_PAD, (h + 1) * HEAD_PAD)
        q_ref[0, h] = (q[:, sl] * cosq + q_rot[:, sl] * sinq).astype(BF16)
        k_ref[0, h] = k[:, sl].astype(BF16)
        vt_ref[0, h, 0] = jnp.where(rows == _ones_row(h), 1.0, vt[sl, :]).astype(BF16)


def _proj_mla(x1, win, gq, gkv, wq, wqr, wk, wv, e2, cosq, sinq, tkr, *, batch, seq, tm, ssm_w):
    n, d = x1.shape
    wcols = win.shape[1]
    hp = MLA_HEADS * HEAD_PAD
    spb = seq // tm
    head_spec = pl.BlockSpec((1, MLA_HEADS, tm, HEAD_PAD), lambda i: (i // spb, 0, i % spb, 0))
    head_shape = jax.ShapeDtypeStruct((batch, MLA_HEADS, seq, HEAD_PAD), BF16)
    vt_spec = pl.BlockSpec((1, MLA_HEADS, 1, HEAD_PAD, tm), lambda i: (i // spb, 0, i % spb, 0, 0))
    vt_shape = jax.ShapeDtypeStruct((batch, MLA_HEADS, spb, HEAD_PAD, tm), BF16)
    tab_spec = pl.BlockSpec((tm, LANES), lambda i: (i % spb, 0))
    groups, chunk_w = ssm_w // SSM_GROUP, S5_CHUNK * SSM_GROUP
    return pl.pallas_call(
        functools.partial(_proj_mla_kernel, d=d, ssm_w=ssm_w),
        grid=(n // tm,),
        in_specs=[pl.BlockSpec((tm, d), lambda i: (i, 0)),
                  _const_spec((d, wcols)),
                  _const_spec((1, Q_LORA)), _const_spec((1, KV_LORA)),
                  _const_spec((Q_LORA, hp)), _const_spec((Q_LORA, hp)),
                  _const_spec((KV_LORA, hp)), _const_spec((hp, KV_LORA)),
                  _const_spec((LANES, hp)),
                  tab_spec, tab_spec, tab_spec],
        out_specs=[head_spec, head_spec, vt_spec,
                   pl.BlockSpec((groups, tm // S5_CHUNK, chunk_w), lambda i: (0, i, 0)),
                   pl.BlockSpec((tm, d), lambda i: (i, 0)),
                   pl.BlockSpec((tm, d), lambda i: (i, 0))],
        out_shape=[head_shape, head_shape, vt_shape,
                   jax.ShapeDtypeStruct((groups, n // S5_CHUNK, chunk_w), BF16),
                   jax.ShapeDtypeStruct((n, d), BF16),
                   jax.ShapeDtypeStruct((n, d), BF16)],
        scratch_shapes=[pltpu.VMEM((ssm_w // LANES, tm, LANES), F32)],
        compiler_params=pltpu.CompilerParams(
            dimension_semantics=("arbitrary",), vmem_limit_bytes=VMEM_LIMIT_BYTES),
        name="proj_mla",
    )(x1, win, gq, gkv, wq, wqr, wk, wv, e2, cosq, sinq, tkr)


def _ones_row(h):
    return V_HEAD if h % 2 == 0 else 0


def _flash_kernel(q_ref, k_ref, vt_ref, o_ref, s_sc, p_sc, a_sc, m_sc, acc_sc, *, tk, nk):
    heads = q_ref.shape[1]
    nbuf = s_sc.shape[1]

    def scores(h, j, slot):
        s_sc[h, slot] = lax.dot_general(
            k_ref[0, h, j * tk:(j + 1) * tk, :], q_ref[0, h], (((1,), (1,)), ((), ())),
            preferred_element_type=F32)

    def softmax(h, slot):
        s = s_sc[h, slot]
        m_prev = m_sc[h]
        m_new = jnp.maximum(m_prev, jnp.max(s, axis=0, keepdims=True))
        a_sc[h, slot] = jnp.exp2(m_prev - m_new)
        p_sc[h, slot] = jnp.exp2(s - m_new).astype(BF16)
        m_sc[h] = m_new

    def weighted_values(h, j, slot):
        acc_sc[h] = a_sc[h, slot] * acc_sc[h] + jnp.dot(
            vt_ref[0, h, j], p_sc[h, slot], preferred_element_type=F32)

    m_sc[...] = jnp.full(m_sc.shape, -jnp.inf, F32)
    acc_sc[...] = jnp.zeros(acc_sc.shape, F32)
    for h in range(heads):
        scores(h, 0, 0)
    for t in range(nk + 1):
        for h in range(heads):
            if t + 1 < nk:
                scores(h, t + 1, (t + 1) % nbuf)
            if t < nk:
                softmax(h, t % nbuf)
            if t >= 1:
                weighted_values(h, t - 1, (t - 1) % nbuf)

    rows = lax.broadcasted_iota(jnp.int32, acc_sc.shape[1:], 0)
    out = jnp.zeros(acc_sc.shape[1:], F32)
    for h in range(heads):
        acc = acc_sc[h]
        l = acc[_ones_row(h):_ones_row(h) + 1]
        out = jnp.where((rows >= h * V_HEAD) & (rows < (h + 1) * V_HEAD), acc / l, out)
    o_ref[0] = out.T.astype(o_ref.dtype)


def _flash_attn(q, k, vt, *, tq):
    b, h, s, dp = q.shape
    nk, tk = vt.shape[2], vt.shape[4]
    hp = HEAD_PAD // V_HEAD
    return pl.pallas_call(
        functools.partial(_flash_kernel, tk=tk, nk=nk),
        grid=(b, h // hp, s // tq),
        in_specs=[pl.BlockSpec((1, hp, tq, dp), lambda bi, hi, qi: (bi, hi, qi, 0)),
                  pl.BlockSpec((1, hp, s, dp), lambda bi, hi, qi: (bi, hi, 0, 0)),
                  pl.BlockSpec((1, hp, nk, dp, tk), lambda bi, hi, qi: (bi, hi, 0, 0, 0))],
        out_specs=pl.BlockSpec((1, tq, dp), lambda bi, hi, qi: (bi, qi, hi)),
        out_shape=jax.ShapeDtypeStruct((b, s, h * V_HEAD), BF16),
        scratch_shapes=[pltpu.VMEM((hp, FLASH_BUFFERS, tk, tq), F32),
                        pltpu.VMEM((hp, FLASH_BUFFERS, tk, tq), BF16),
                        pltpu.VMEM((hp, FLASH_BUFFERS, 1, tq), F32),
                        pltpu.VMEM((hp, 1, tq), F32),
                        pltpu.VMEM((hp, dp, tq), F32)],
        compiler_params=pltpu.CompilerParams(
            dimension_semantics=("arbitrary", "arbitrary", "arbitrary"),
            vmem_limit_bytes=VMEM_LIMIT_BYTES),
        name="flash_attn",
    )(q, k, vt)


def _seg_shift(a, step, pos, seg, is_fwd):
    n, w = a.shape
    if seg == SUBLANES:
        a3 = a.reshape(n // seg, seg, w)
        down = pltpu.roll(a3, step, 1).reshape(n, w)
        up = pltpu.roll(a3, seg - step, 1).reshape(n, w)
    else:
        down, up = pltpu.roll(a, step, 0), pltpu.roll(a, n - step, 0)
    return jnp.where(is_fwd, jnp.where(pos >= step, down, 0.0), jnp.where(pos < seg - step, up, 0.0))


def _decayed_scan(re, im, dec_ref, first_pow, seg, is_fwd):
    pos = lax.broadcasted_iota(jnp.int32, re.shape, 0) % seg
    step, i = 1, first_pow
    while step < seg:
        dr, di = dec_ref[0, i:i + 1, :], dec_ref[1, i:i + 1, :]
        sr = _seg_shift(re, step, pos, seg, is_fwd)
        si = _seg_shift(im, step, pos, seg, is_fwd)
        re, im = re + dr * sr - di * si, im + dr * si + di * sr
        step, i = 2 * step, i + 1
    return re, im


def _s5_kernel(u_ref, t_ref, ws_ref, wo_ref, dec_ref, pow_ref, exp_ref, z_ref, l_sc, *, gb, nk):
    w = 2 * SSM_STATE
    r = u_ref.shape[1]
    tiles = r // SUBLANES
    planes = [jnp.dot(u_ref[g], ws_ref[g], preferred_element_type=F32) for g in range(gb)]
    s_re = jnp.concatenate([a[:, :w] for a in planes], axis=1)
    s_im = jnp.concatenate([a[:, w:] for a in planes], axis=1)
    fwd = lambda rows: (lax.broadcasted_iota(jnp.int32, (rows, gb * w), 1) % w) < SSM_STATE

    l_re, l_im = _decayed_scan(s_re, s_im, dec_ref, 0, SUBLANES, fwd(r))

    for j in range(gb):
        l_sc[0, j] = l_re[:, j * w:(j + 1) * w]
        l_sc[1, j] = l_im[:, j * w:(j + 1) * w]
    fwd_w = lax.broadcasted_iota(jnp.int32, (tiles, w), 1) < SSM_STATE
    ends = [jnp.concatenate(
        [jnp.where(fwd_w, l_sc[c, j, pl.ds(SUBLANES - 1, tiles, stride=SUBLANES), :],
                   l_sc[c, j, pl.ds(0, tiles, stride=SUBLANES), :]) for j in range(gb)], axis=1)
        for c in range(2)]
    seg2 = nk // SUBLANES
    pos2 = lax.broadcasted_iota(jnp.int32, ends[0].shape, 0) % seg2
    e_re = _seg_shift(ends[0], 1, pos2, seg2, fwd(tiles))
    e_im = _seg_shift(ends[1], 1, pos2, seg2, fwd(tiles))
    c_re, c_im = _decayed_scan(e_re, e_im, dec_ref, 3, seg2, fwd(tiles))

    z_re = jnp.dot(exp_ref[...], c_re.astype(BF16), preferred_element_type=F32)
    z_im = jnp.dot(exp_ref[...], c_im.astype(BF16), preferred_element_type=F32)
    p_re = jnp.broadcast_to(pow_ref[0][None], (tiles, SUBLANES, gb * w)).reshape(r, gb * w)
    p_im = jnp.broadcast_to(pow_ref[1][None], (tiles, SUBLANES, gb * w)).reshape(r, gb * w)
    pos1 = lax.broadcasted_iota(jnp.int32, (r, gb * w), 0) % SUBLANES
    x_re = _seg_shift(l_re, 1, pos1, SUBLANES, fwd(r)) + p_re * z_re - p_im * z_im
    x_im = _seg_shift(l_im, 1, pos1, SUBLANES, fwd(r)) + p_re * z_im + p_im * z_re

    for g in range(gb):
        sl = slice(g * w, (g + 1) * w)
        xin = jnp.concatenate([x_re[:, sl], x_im[:, sl]], axis=1).astype(BF16)
        y = (jnp.dot(u_ref[g], t_ref[g], preferred_element_type=F32)
             + jnp.dot(xin, wo_ref[g], preferred_element_type=F32))
        z_ref[g] = jax.nn.gelu(y).astype(z_ref.dtype)


def _s5_scan(u5, t_op, ws, wo, dec, pows, expand, *, nk, gb):
    g, r, cw = u5.shape
    w = 2 * SSM_STATE
    grp = lambda i: (i, 0, 0)
    par = lambda a: pl.BlockSpec((None,) + a.shape[1:], lambda i: (i, 0, 0, 0))
    return pl.pallas_call(
        functools.partial(_s5_kernel, gb=gb, nk=nk),
        grid=(g // gb,),
        in_specs=[pl.BlockSpec((gb, r, cw), grp),
                  pl.BlockSpec((gb, cw, cw), grp), pl.BlockSpec((gb, cw, 2 * w), grp),
                  pl.BlockSpec((gb, 2 * w, cw), grp),
                  par(dec), par(pows), _const_spec(expand.shape)],
        out_specs=pl.BlockSpec((gb, r, cw), grp),
        out_shape=jax.ShapeDtypeStruct((g, r, cw), BF16),
        scratch_shapes=[pltpu.VMEM((2, gb, r, w), F32)],
        compiler_params=pltpu.CompilerParams(
            dimension_semantics=("arbitrary",), vmem_limit_bytes=VMEM_LIMIT_BYTES),
        name="s5_scan",
    )(u5, t_op, ws, wo, dec, pows, expand)


def _s5_operators(fwd, bwd, d_skip, n_doublings):
    L = S5_CHUNK
    hi = lax.Precision.HIGHEST

    def disc(lam_re, lam_im, log_dt, b_re, b_im, c_re, c_im):
        lr, li = lam_re.astype(F32), lam_im.astype(F32)
        dt = jnp.exp(log_dt.astype(F32))[:, None]
        steps = jnp.arange(L + 1, dtype=F32)
        mag = jnp.exp((lr * dt)[..., None] * steps)
        ang = (li * dt)[..., None] * steps
        pw_re, pw_im = mag * jnp.cos(ang), mag * jnp.sin(ang)
        xr, xi = pw_re[..., 1] - 1.0, pw_im[..., 1]
        den = lr * lr + li * li
        kr, ki = (xr * lr + xi * li) / den, (xi * lr - xr * li) / den
        br, bi = b_re.astype(F32), b_im.astype(F32)
        bb_re = kr[..., None] * br - ki[..., None] * bi
        bb_im = kr[..., None] * bi + ki[..., None] * br
        e_re = pw_re[..., :L, None] * bb_re[:, :, None, :] - pw_im[..., :L, None] * bb_im[:, :, None, :]
        e_im = pw_re[..., :L, None] * bb_im[:, :, None, :] + pw_im[..., :L, None] * bb_re[:, :, None, :]
        cr, ci = c_re.astype(F32), c_im.astype(F32)
        taps = (jnp.einsum('gcp,gptd->gtcd', cr, e_re, precision=hi)
                - jnp.einsum('gcp,gptd->gtcd', ci, e_im, precision=hi))
        return pw_re, pw_im, e_re, e_im, cr, ci, taps

    pwr_f, pwi_f, er_f, ei_f, cr_f, ci_f, taps_f = disc(*fwd)
    pwr_b, pwi_b, er_b, ei_b, cr_b, ci_b, taps_b = disc(*bwd)
    G, P = pwr_f.shape[:2]
    C = er_f.shape[-1]
    zeros = lambda m: jnp.zeros((G, m, C, C), F32)
    tf = jnp.stack([jnp.concatenate([zeros(s), taps_f[:, :L - s]], axis=1) for s in range(L)], axis=1)
    tb = jnp.stack([jnp.concatenate([taps_b[:, s::-1], zeros(L - 1 - s)], axis=1) for s in range(L)], axis=1)
    skip = (jnp.eye(L, dtype=F32)[None, :, :, None, None]
            * (jnp.eye(C, dtype=F32)[None, None, None] * d_skip.astype(F32)[:, None, None, :, None]))
    t_op = (tf + tb + skip).transpose(0, 1, 4, 2, 3).reshape(G, L * C, L * C)

    to_sdp = lambda e: e.transpose(0, 2, 3, 1)
    ws = jnp.concatenate(
        [to_sdp(er_f[:, :, ::-1]), to_sdp(er_b), to_sdp(ei_f[:, :, ::-1]), to_sdp(ei_b)],
        axis=-1).reshape(G, L * C, 4 * P)

    def out_map(cr, ci, pr, pi):
        c_r, c_i = cr.transpose(0, 2, 1)[:, :, None, :], ci.transpose(0, 2, 1)[:, :, None, :]
        p_r, p_i = pr[..., None], pi[..., None]
        return ((c_r * p_r - c_i * p_i).reshape(G, P, L * C),
                (c_r * p_i + c_i * p_r).reshape(G, P, L * C))

    of_re, of_im = out_map(cr_f, ci_f, pwr_f[..., 1:L + 1], pwi_f[..., 1:L + 1])
    ob_re, ob_im = out_map(cr_b, ci_b, pwr_b[..., L:0:-1], pwi_b[..., L:0:-1])
    wo = jnp.concatenate([of_re, ob_re, -of_im, -ob_im], axis=1)

    def chunk_pow(lam_re, lam_im, log_dt, mult):
        dt = jnp.exp(log_dt.astype(F32))[:, None]
        mag = jnp.exp((lam_re.astype(F32) * dt)[:, None, :] * (L * mult)[None, :, None])
        ang = (lam_im.astype(F32) * dt)[:, None, :] * (L * mult)[None, :, None]
        return mag * jnp.cos(ang), mag * jnp.sin(ang)

    doubling = 2.0 ** jnp.arange(n_doublings, dtype=F32)
    in_tile = jnp.arange(SUBLANES, dtype=F32)
    dec = [jnp.concatenate([f, b], axis=-1) for f, b in
           zip(chunk_pow(*fwd[:3], doubling), chunk_pow(*bwd[:3], doubling))]
    pows = [jnp.concatenate([f, b], axis=-1) for f, b in
            zip(chunk_pow(*fwd[:3], in_tile), chunk_pow(*bwd[:3], in_tile[::-1]))]
    return (t_op.astype(BF16), ws.astype(BF16), wo.astype(BF16),
            jnp.stack(dec, axis=1), jnp.stack(pows, axis=1))


def _merge_ln_kernel(x_ref, o_ref, z_ref, sa_ref, sb_ref, woa_ref, wglu_ref, wos_ref, wout_ref,
                     g_ref, b_ref, out_ref, z_sc):
    y_a = jnp.dot(o_ref[...], woa_ref[...], preferred_element_type=F32)
    z = _from_chunk_layout(z_ref, z_sc)
    gl = jax.nn.sigmoid(jnp.dot(z.astype(BF16), wglu_ref[...], preferred_element_type=F32))
    y_b = jnp.dot((z * gl).astype(BF16), wos_ref[...], preferred_element_type=F32)
    merged = (sa_ref[...].astype(F32) * y_a + sb_ref[...].astype(F32) * y_b).astype(BF16)
    r = ALPHA * x_ref[...] + jnp.dot(merged, wout_ref[...], preferred_element_type=F32)
    out_ref[...] = _layer_norm(r, g_ref[...], b_ref[...])


def _merge_ln(x1, o, z, sa, sb, woa, wglu, wos, wout, g, b, *, seq, tm):
    n, d = x1.shape
    groups, _, chunk_w = z.shape
    spb = seq // tm
    return pl.pallas_call(
        _merge_ln_kernel,
        grid=(n // tm,),
        in_specs=[pl.BlockSpec((tm, d), lambda i: (i, 0)),
                  pl.BlockSpec((tm, o.shape[1]), lambda i: (i, 0)),
                  pl.BlockSpec((groups, tm // S5_CHUNK, chunk_w), lambda i: (0, i, 0)),
                  pl.BlockSpec((tm, d), lambda i: (i, 0)),
                  pl.BlockSpec((tm, d), lambda i: (i, 0)),
                  _const_spec(woa.shape), _const_spec(wglu.shape), _const_spec(wos.shape),
                  _const_spec(wout.shape), _const_spec((1, d)), _const_spec((1, d))],
        out_specs=pl.BlockSpec((tm, d), lambda i: (i, 0)),
        out_shape=jax.ShapeDtypeStruct((n, d), F32),
        scratch_shapes=[pltpu.VMEM((groups * SSM_GROUP // LANES, tm, LANES), F32)],
        compiler_params=pltpu.CompilerParams(
            dimension_semantics=("arbitrary",), vmem_limit_bytes=VMEM_LIMIT_BYTES),
        name="merge_ln",
    )(x1, o, z, sa, sb, woa, wglu, wos, wout, g, b)


def _rot_cols(w):
    half = w.shape[-1] // 2
    return jnp.concatenate([-w[..., half:], w[..., :half]], axis=-1)


def _pad_heads(w, offset):
    r, h, dh = w.shape
    cols = [jnp.zeros((r, HEAD_PAD), w.dtype).at[:, offset(hi):offset(hi) + dh].set(w[:, hi])
            for hi in range(h)]
    return jnp.concatenate(cols, axis=1)


def kernel(x, p, ffn1_w1, ffn1_w3, ffn1_w2, ln1_g, ln1_b, w_in, q_norm_g, kv_norm_g, w_uq, w_uk, w_uv, w_o_attn, ssm_lam_re_f, ssm_lam_im_f, ssm_log_dt_f, ssm_b_re_f, ssm_b_im_f, ssm_c_re_f, ssm_c_im_f, ssm_lam_re_b, ssm_lam_im_b, ssm_log_dt_b, ssm_b_re_b, ssm_b_im_b, ssm_c_re_b, ssm_c_im_b, ssm_d, w_glu, w_o_ssm, w_out, ln2_g, ln2_b, ffn2_w1, ffn2_w3, ffn2_w2, ln3_g, ln3_b, ple_w_proj, ple_w_gate, ln4_g, ln4_b):
    batch, seq, d = x.shape
    n = batch * seq
    ssm_w = w_glu.shape[1]
    groups = ssm_w // SSM_GROUP
    assert DEPTH == 1 and ffn1_w1.shape[0] == 1
    assert seq % (SUBLANES * S5_CHUNK) == 0 and groups % S5_GROUP_BLOCK == 0
    assert ssm_w % LANES == 0
    tm = min(512, seq)
    tq = min(1024, seq)
    i = 0
    row = lambda a: a[i].reshape(1, -1).astype(F32)

    x0 = x.reshape(n, d)
    x1 = _ffn_ln(x0, ffn1_w1[i].astype(BF16), ffn1_w3[i].astype(BF16), ffn1_w2[i].astype(BF16),
                 row(ln1_g), row(ln1_b), tm=tm)

    o_kv = Q_LORA
    o_kr = o_kv + KV_LORA
    o_u = o_kr + QK_ROPE
    o_ga = o_u + ssm_w
    o_gb = o_ga + d
    w = w_in[i]
    w_kr = w[:, o_kr:o_u]
    kr_block = jnp.concatenate(
        [w_kr, _rot_cols(w_kr), jnp.zeros((d, LANES - 2 * QK_ROPE), w.dtype)], axis=1)
    win_ext = jnp.concatenate(
        [w[:, :o_kr], w[:, o_u:o_ga], w[:, o_ga:o_gb], w[:, o_gb:], kr_block], axis=1).astype(BF16)
    wq = w_uq[i]
    wq_rot = jnp.concatenate(
        [jnp.zeros_like(wq[..., :QK_NOPE]), _rot_cols(wq[..., QK_NOPE:])], axis=-1)
    wq_p = _pad_heads(wq, lambda h: 0).astype(BF16)
    wqr_p = _pad_heads(wq_rot, lambda h: 0).astype(BF16)
    wk_p = _pad_heads(w_uk[i], lambda h: 0).astype(BF16)
    wv_pt = _pad_heads(w_uv[i], lambda h: (h % 2) * V_HEAD).T.astype(BF16)
    j = jnp.arange(QK_ROPE)
    e2 = jnp.zeros((LANES, MLA_HEADS, HEAD_PAD), F32)
    e2 = e2.at[j, :, QK_NOPE + j].set(1.0).at[QK_ROPE + j, :, QK_NOPE + j].set(1.0)
    e2 = e2.reshape(LANES, MLA_HEADS * HEAD_PAD).astype(BF16)
    pos = jnp.arange(seq, dtype=F32)
    inv_freq = ROPE_THETA ** (-jnp.arange(0, QK_ROPE, 2, dtype=F32) / QK_ROPE)
    ang = pos[:, None] * inv_freq[None, :]
    cos, sin = jnp.cos(ang), jnp.sin(ang)
    scale = (QK_NOPE + QK_ROPE) ** -0.5 * math.log2(math.e)
    pad_q = jnp.zeros((seq, HEAD_PAD - QK_NOPE - QK_ROPE), F32)
    cosq = jnp.concatenate([jnp.ones((seq, QK_NOPE), F32), cos, cos, pad_q], axis=1) * scale
    sinq = jnp.concatenate([jnp.zeros((seq, QK_NOPE), F32), sin, sin, pad_q], axis=1) * scale
    tkr = jnp.concatenate([cos, cos, sin, sin, jnp.zeros((seq, LANES - 2 * QK_ROPE), F32)], axis=1)

    q, k, vt, u5, sa, sb = _proj_mla(
        x1, win_ext, row(q_norm_g), row(kv_norm_g), wq_p, wqr_p, wk_p, wv_pt, e2, cosq, sinq, tkr,
        batch=batch, seq=seq, tm=tm, ssm_w=ssm_w)

    o = _flash_attn(q, k, vt, tq=tq).reshape(n, -1)

    nk = seq // S5_CHUNK
    fwd = (ssm_lam_re_f[i], ssm_lam_im_f[i], ssm_log_dt_f[i], ssm_b_re_f[i], ssm_b_im_f[i],
           ssm_c_re_f[i], ssm_c_im_f[i])
    bwd = (ssm_lam_re_b[i], ssm_lam_im_b[i], ssm_log_dt_b[i], ssm_b_re_b[i], ssm_b_im_b[i],
           ssm_c_re_b[i], ssm_c_im_b[i])
    tiles_per_seq = nk // SUBLANES
    n_doublings = 3 + max(0, (tiles_per_seq - 1).bit_length())
    t_op, ws, wo, dec, pows = _s5_operators(
        fwd, bwd, ssm_d[i].reshape(groups, SSM_GROUP), n_doublings)
    gb = S5_GROUP_BLOCK
    per_block = lambda a: (a.reshape(groups // gb, gb, *a.shape[1:]).transpose(0, 2, 3, 1, 4)
                           .reshape(groups // gb, a.shape[1], a.shape[2], gb * a.shape[3]))
    rows = n // S5_CHUNK
    expand = (jnp.arange(rows)[:, None] // SUBLANES == jnp.arange(rows // SUBLANES)[None, :]).astype(BF16)
    z5 = _s5_scan(u5, t_op, ws, wo, per_block(dec), per_block(pows), expand, nk=nk, gb=gb)

    x2 = _merge_ln(x1, o, z5, sa, sb, w_o_attn[i].astype(BF16), w_glu[i].astype(BF16),
                   w_o_ssm[i].astype(BF16), w_out[i].astype(BF16), row(ln2_g), row(ln2_b),
                   seq=seq, tm=tm)

    x4 = _ffn_ple_ln(x2, p[i].reshape(n, -1), ffn2_w1[i].astype(BF16), ffn2_w3[i].astype(BF16),
                     ffn2_w2[i].astype(BF16), row(ln3_g), row(ln3_b),
                     ple_w_gate[i].astype(BF16), ple_w_proj[i].astype(BF16),
                     row(ln4_g), row(ln4_b), tm=tm)
    return x4.reshape(batch, seq, d)
```

```python
import functools
import math

import jax
import jax.numpy as jnp
from jax import lax
from jax.experimental import pallas as pl
from jax.experimental.pallas import tpu as pltpu

F32 = jnp.float32
BF16 = jnp.bfloat16

DEPTH = 1
MLA_HEADS = 8
QK_NOPE = 64
QK_ROPE = 32
V_HEAD = 64
Q_LORA = 384
KV_LORA = 256
ROPE_THETA = 10000.0
SSM_GROUP = 16
SSM_STATE = 64
LN_EPS = 1e-5
RMS_EPS = 1e-6
ALPHA = (2.0 * DEPTH) ** 0.25

LANES = 128
SUBLANES = 8
VMEM_LIMIT_BYTES = 56 * 1024 * 1024

HEAD_PAD = LANES
S5_CHUNK = 16
S5_GROUP_BLOCK = 2
FLASH_BUFFERS = 2


def _const_spec(shape):
    nd = len(shape)
    return pl.BlockSpec(shape, lambda *_: (0,) * nd, pipeline_mode=pl.Buffered(1))


def _layer_norm(r, g, b):
    mu = jnp.mean(r, axis=-1, keepdims=True)
    c = r - mu
    var = jnp.mean(c * c, axis=-1, keepdims=True)
    return c * lax.rsqrt(var + LN_EPS) * g + b


def _rms_norm(x, g):
    return x * lax.rsqrt(jnp.mean(x * x, axis=-1, keepdims=True) + RMS_EPS) * g


def _swiglu(xb, w1_ref, w3_ref, w2_ref):
    h1 = jnp.dot(xb, w1_ref[...], preferred_element_type=F32)
    h3 = jnp.dot(xb, w3_ref[...], preferred_element_type=F32)
    a = (h1 * jax.nn.sigmoid(h1) * h3).astype(BF16)
    return jnp.dot(a, w2_ref[...], preferred_element_type=F32)


def _ffn_ln_kernel(x_ref, w1_ref, w3_ref, w2_ref, g_ref, b_ref, o_ref):
    x = x_ref[...]
    y = _swiglu(x.astype(BF16), w1_ref, w3_ref, w2_ref)
    o_ref[...] = _layer_norm(ALPHA * x + 0.5 * y, g_ref[...], b_ref[...])


def _ffn_ln(x, w1, w3, w2, g, b, *, tm):
    n, d = x.shape
    f = w1.shape[1]
    return pl.pallas_call(
        _ffn_ln_kernel,
        grid=(n // tm,),
        in_specs=[pl.BlockSpec((tm, d), lambda i: (i, 0)),
                  _const_spec((d, f)), _const_spec((d, f)), _const_spec((f, d)),
                  _const_spec((1, d)), _const_spec((1, d))],
        out_specs=pl.BlockSpec((tm, d), lambda i: (i, 0)),
        out_shape=jax.ShapeDtypeStruct((n, d), F32),
        compiler_params=pltpu.CompilerParams(
            dimension_semantics=("arbitrary",), vmem_limit_bytes=VMEM_LIMIT_BYTES),
        name="ffn_ln",
    )(x, w1, w3, w2, g, b)


def _ffn_ple_ln_kernel(x_ref, p_ref, w1_ref, w3_ref, w2_ref, g3_ref, b3_ref,
                       wpg_ref, wpp_ref, g4_ref, b4_ref, o_ref):
    x = x_ref[...]
    y = _swiglu(x.astype(BF16), w1_ref, w3_ref, w2_ref)
    x3 = _layer_norm(ALPHA * x + 0.5 * y, g3_ref[...], b3_ref[...])
    gate = jax.nn.sigmoid(jnp.dot(x3.astype(BF16), wpg_ref[...], preferred_element_type=F32))
    emb = jnp.dot(p_ref[...].astype(BF16), wpp_ref[...], preferred_element_type=F32)
    o_ref[...] = _layer_norm(ALPHA * x3 + gate * emb, g4_ref[...], b4_ref[...])


def _ffn_ple_ln(x, p, w1, w3, w2, g3, b3, wpg, wpp, g4, b4, *, tm):
    n, d = x.shape
    f = w1.shape[1]
    pd = p.shape[1]
    return pl.pallas_call(
        _ffn_ple_ln_kernel,
        grid=(n // tm,),
        in_specs=[pl.BlockSpec((tm, d), lambda i: (i, 0)),
                  pl.BlockSpec((tm, pd), lambda i: (i, 0)),
                  _const_spec((d, f)), _const_spec((d, f)), _const_spec((f, d)),
                  _const_spec((1, d)), _const_spec((1, d)),
                  _const_spec((d, d)), _const_spec((pd, d)),
                  _const_spec((1, d)), _const_spec((1, d))],
        out_specs=pl.BlockSpec((tm, d), lambda i: (i, 0)),
        out_shape=jax.ShapeDtypeStruct((n, d), F32),
        compiler_params=pltpu.CompilerParams(
            dimension_semantics=("arbitrary",), vmem_limit_bytes=VMEM_LIMIT_BYTES),
        name="ffn_ple_ln",
    )(x, p, w1, w3, w2, g3, b3, wpg, wpp, g4, b4)


GROUPS_PER_TILE = LANES // SSM_GROUP


def _to_chunk_layout(u, out_ref, sc):
    tm = u.shape[0]
    nc = tm // S5_CHUNK
    for j in range(sc.shape[0]):
        sc[j] = u[:, j * LANES:(j + 1) * LANES]
        by_t = [sc[j, pl.ds(t, nc, stride=S5_CHUNK), :] for t in range(S5_CHUNK)]
        for i in range(GROUPS_PER_TILE):
            out_ref[j * GROUPS_PER_TILE + i] = jnp.concatenate(
                [a[:, i * SSM_GROUP:(i + 1) * SSM_GROUP] for a in by_t], axis=1).astype(out_ref.dtype)


def _from_chunk_layout(z_ref, sc):
    nc = z_ref.shape[1]
    for j in range(sc.shape[0]):
        zs = [z_ref[j * GROUPS_PER_TILE + i].astype(F32) for i in range(GROUPS_PER_TILE)]
        for t in range(S5_CHUNK):
            sc[j, pl.ds(t, nc, stride=S5_CHUNK), :] = jnp.concatenate(
                [a[:, t * SSM_GROUP:(t + 1) * SSM_GROUP] for a in zs], axis=1)
    return jnp.concatenate([sc[j] for j in range(sc.shape[0])], axis=1)


def _proj_mla_kernel(x_ref, win_ref, gq_ref, gkv_ref, wq_ref, wqr_ref, wk_ref, wv_ref,
                     e_ref, cosq_ref, sinq_ref, tk_ref,
                     q_ref, k_ref, vt_ref, u_ref, sa_ref, sb_ref, u_sc, *, d, ssm_w):
    xb = x_ref[...].astype(BF16)
    proj = jnp.dot(xb, win_ref[...], preferred_element_type=F32)
    o_kv = Q_LORA
    o_u = o_kv + KV_LORA
    o_ga = o_u + ssm_w
    o_gb = o_ga + d
    o_kr = o_gb + d
    c_q = _rms_norm(proj[:, :Q_LORA], gq_ref[...]).astype(BF16)
    c_kv = _rms_norm(proj[:, o_kv:o_u], gkv_ref[...]).astype(BF16)
    _to_chunk_layout(proj[:, o_u:o_ga], u_ref, u_sc)
    sa_ref[...] = jax.nn.sigmoid(proj[:, o_ga:o_gb]).astype(BF16)
    sb_ref[...] = jax.nn.sigmoid(proj[:, o_gb:o_kr]).astype(BF16)
    kr = (proj[:, o_kr:o_kr + LANES] * tk_ref[...]).astype(BF16)
    q = jnp.dot(c_q, wq_ref[...], preferred_element_type=F32)
    q_rot = jnp.dot(c_q, wqr_ref[...], preferred_element_type=F32)
    k = (jnp.dot(c_kv, wk_ref[...], preferred_element_type=F32)
         + jnp.dot(kr, e_ref[...], preferred_element_type=F32))
    vt = lax.dot_general(wv_ref[...], c_kv, (((1,), (1,)), ((), ())), preferred_element_type=F32)
    cosq = cosq_ref[...]
    sinq = sinq_ref[...]
    rows = lax.broadcasted_iota(jnp.int32, (HEAD_PAD, vt.shape[1]), 0)
    for h in range(MLA_HEADS):
        sl = slice(h * HEAD_PAD, (h + 1) * HEAD_PAD)
        q_ref[0, h] = (q[:, sl] * cosq + q_rot[:, sl] * sinq).astype(BF16)
        k_ref[0, h] = k[:, sl].astype(BF16)
        vt_ref[0, h, 0] = jnp.where(rows == _ones_row(h), 1.0, vt[sl, :]).astype(BF16)


def _proj_mla(x1, win, gq, gkv, wq, wqr, wk, wv, e2, cosq, sinq, tkr, *, batch, seq, tm, ssm_w):
    n, d = x1.shape
    wcols = win.shape[1]
    hp = MLA_HEADS * HEAD_PAD
    spb = seq // tm
    head_spec = pl.BlockSpec((1, MLA_HEADS, tm, HEAD_PAD), lambda i: (i // spb, 0, i % spb, 0))
    head_shape = jax.ShapeDtypeStruct((batch, MLA_HEADS, seq, HEAD_PAD), BF16)
    vt_spec = pl.BlockSpec((1, MLA_HEADS, 1, HEAD_PAD, tm), lambda i: (i // spb, 0, i % spb, 0, 0))
    vt_shape = jax.ShapeDtypeStruct((batch, MLA_HEADS, spb, HEAD_PAD, tm), BF16)
    tab_spec = pl.BlockSpec((tm, LANES), lambda i: (i % spb, 0))
    groups, chunk_w = ssm_w // SSM_GROUP, S5_CHUNK * SSM_GROUP
    return pl.pallas_call(
        functools.partial(_proj_mla_kernel, d=d, ssm_w=ssm_w),
        grid=(n // tm,),
        in_specs=[pl.BlockSpec((tm, d), lambda i: (i, 0)),
                  _const_spec((d, wcols)),
                  _const_spec((1, Q_LORA)), _const_spec((1, KV_LORA)),
                  _const_spec((Q_LORA, hp)), _const_spec((Q_LORA, hp)),
                  _const_spec((KV_LORA, hp)), _const_spec((hp, KV_LORA)),
                  _const_spec((LANES, hp)),
                  tab_spec, tab_spec, tab_spec],
        out_specs=[head_spec, head_spec, vt_spec,
                   pl.BlockSpec((groups, tm // S5_CHUNK, chunk_w), lambda i: (0, i, 0)),
                   pl.BlockSpec((tm, d), lambda i: (i, 0)),
                   pl.BlockSpec((tm, d), lambda i: (i, 0))],
        out_shape=[head_shape, head_shape, vt_shape,
                   jax.ShapeDtypeStruct((groups, n // S5_CHUNK, chunk_w), BF16),
                   jax.ShapeDtypeStruct((n, d), BF16),
                   jax.ShapeDtypeStruct((n, d), BF16)],
        scratch_shapes=[pltpu.VMEM((ssm_w // LANES, tm, LANES), F32)],
        compiler_params=pltpu.CompilerParams(
            dimension_semantics=("arbitrary",), vmem_limit_bytes=VMEM_LIMIT_BYTES),
        name="proj_mla",
    )(x1, win, gq, gkv, wq, wqr, wk, wv, e2, cosq, sinq, tkr)


def _ones_row(h):
    return V_HEAD if h % 2 == 0 else 0


def _flash_kernel(q_ref, k_ref, vt_ref, o_ref, s_sc, p_sc, a_sc, m_sc, acc_sc, *, tk, nk):
    heads = q_ref.shape[1]
    nbuf = s_sc.shape[1]

    def scores(h, j, slot):
        s_sc[h, slot] = lax.dot_general(
            k_ref[0, h, j * tk:(j + 1) * tk, :], q_ref[0, h], (((1,), (1,)), ((), ())),
            preferred_element_type=F32)

    def softmax(h, slot):
        s = s_sc[h, slot]
        m_prev = m_sc[h]
        m_new = jnp.maximum(m_prev, jnp.max(s, axis=0, keepdims=True))
        a_sc[h, slot] = jnp.exp2(m_prev - m_new)
        p_sc[h, slot] = jnp.exp2(s - m_new).astype(BF16)
        m_sc[h] = m_new

    def weighted_values(h, j, slot):
        acc_sc[h] = a_sc[h, slot] * acc_sc[h] + jnp.dot(
            vt_ref[0, h, j], p_sc[h, slot], preferred_element_type=F32)

    m_sc[...] = jnp.full(m_sc.shape, -jnp.inf, F32)
    acc_sc[...] = jnp.zeros(acc_sc.shape, F32)
    for h in range(heads):
        scores(h, 0, 0)
    for t in range(nk + 1):
        for h in range(heads):
            if t + 1 < nk:
                scores(h, t + 1, (t + 1) % nbuf)
            if t < nk:
                softmax(h, t % nbuf)
            if t >= 1:
                weighted_values(h, t - 1, (t - 1) % nbuf)

    rows = lax.broadcasted_iota(jnp.int32, acc_sc.shape[1:], 0)
    out = jnp.zeros(acc_sc.shape[1:], F32)
    for h in range(heads):
        acc = acc_sc[h]
        l = acc[_ones_row(h):_ones_row(h) + 1]
        out = jnp.where((rows >= h * V_HEAD) & (rows < (h + 1) * V_HEAD), acc / l, out)
    o_ref[0] = out.T.astype(o_ref.dtype)


def _flash_attn(q, k, vt, *, tq):
    b, h, s, dp = q.shape
    nk, tk = vt.shape[2], vt.shape[4]
    hp = HEAD_PAD // V_HEAD
    return pl.pallas_call(
        functools.partial(_flash_kernel, tk=tk, nk=nk),
        grid=(b, h // hp, s // tq),
        in_specs=[pl.BlockSpec((1, hp, tq, dp), lambda bi, hi, qi: (bi, hi, qi, 0)),
                  pl.BlockSpec((1, hp, s, dp), lambda bi, hi, qi: (bi, hi, 0, 0)),
                  pl.BlockSpec((1, hp, nk, dp, tk), lambda bi, hi, qi: (bi, hi, 0, 0, 0))],
        out_specs=pl.BlockSpec((1, tq, dp), lambda bi, hi, qi: (bi, qi, hi)),
        out_shape=jax.ShapeDtypeStruct((b, s, h * V_HEAD), BF16),
        scratch_shapes=[pltpu.VMEM((hp, FLASH_BUFFERS, tk, tq), F32),
                        pltpu.VMEM((hp, FLASH_BUFFERS, tk, tq), BF16),
                        pltpu.VMEM((hp, FLASH_BUFFERS, 1, tq), F32),
                        pltpu.VMEM((hp, 1, tq), F32),
                        pltpu.VMEM((hp, dp, tq), F32)],
        compiler_params=pltpu.CompilerParams(
            dimension_semantics=("arbitrary", "arbitrary", "arbitrary"),
            vmem_limit_bytes=VMEM_LIMIT_BYTES),
        name="flash_attn",
    )(q, k, vt)


def _seg_shift(a, step, pos, seg, is_fwd):
    n, w = a.shape
    if seg == SUBLANES:
        a3 = a.reshape(n // seg, seg, w)
        down = pltpu.roll(a3, step, 1).reshape(n, w)
        up = pltpu.roll(a3, seg - step, 1).reshape(n, w)
    else:
        down, up = pltpu.roll(a, step, 0), pltpu.roll(a, n - step, 0)
    return jnp.where(is_fwd, jnp.where(pos >= step, down, 0.0), jnp.where(pos < seg - step, up, 0.0))


def _decayed_scan(re, im, dec_ref, first_pow, seg, is_fwd):
    pos = lax.broadcasted_iota(jnp.int32, re.shape, 0) % seg
    step, i = 1, first_pow
    while step < seg:
        dr, di = dec_ref[0, i:i + 1, :], dec_ref[1, i:i + 1, :]
        sr = _seg_shift(re, step, pos, seg, is_fwd)
        si = _seg_shift(im, step, pos, seg, is_fwd)
        re, im = re + dr * sr - di * si, im + dr * si + di * sr
        step, i = 2 * step, i + 1
    return re, im


def _s5_kernel(u_ref, t_ref, ws_ref, wo_ref, dec_ref, pow_ref, exp_ref, z_ref, l_sc, *, gb, nk):
    w = 2 * SSM_STATE
    r = u_ref.shape[1]
    tiles = r // SUBLANES
    planes = [jnp.dot(u_ref[g], ws_ref[g], preferred_element_type=F32) for g in range(gb)]
    s_re = jnp.concatenate([a[:, :w] for a in planes], axis=1)
    s_im = jnp.concatenate([a[:, w:] for a in planes], axis=1)
    fwd = lambda rows: (lax.broadcasted_iota(jnp.int32, (rows, gb * w), 1) % w) < SSM_STATE

    l_re, l_im = _decayed_scan(s_re, s_im, dec_ref, 0, SUBLANES, fwd(r))

    for j in range(gb):
        l_sc[0, j] = l_re[:, j * w:(j + 1) * w]
        l_sc[1, j] = l_im[:, j * w:(j + 1) * w]
    fwd_w = lax.broadcasted_iota(jnp.int32, (tiles, w), 1) < SSM_STATE
    ends = [jnp.concatenate(
        [jnp.where(fwd_w, l_sc[c, j, pl.ds(SUBLANES - 1, tiles, stride=SUBLANES), :],
                   l_sc[c, j, pl.ds(0, tiles, stride=SUBLANES), :]) for j in range(gb)], axis=1)
        for c in range(2)]
    seg2 = nk // SUBLANES
    pos2 = lax.broadcasted_iota(jnp.int32, ends[0].shape, 0) % seg2
    e_re = _seg_shift(ends[0], 1, pos2, seg2, fwd(tiles))
    e_im = _seg_shift(ends[1], 1, pos2, seg2, fwd(tiles))
    c_re, c_im = _decayed_scan(e_re, e_im, dec_ref, 3, seg2, fwd(tiles))

    z_re = jnp.dot(exp_ref[...], c_re.astype(BF16), preferred_element_type=F32)
    z_im = jnp.dot(exp_ref[...], c_im.astype(BF16), preferred_element_type=F32)
    p_re = jnp.broadcast_to(pow_ref[0][None], (tiles, SUBLANES, gb * w)).reshape(r, gb * w)
    p_im = jnp.broadcast_to(pow_ref[1][None], (tiles, SUBLANES, gb * w)).reshape(r, gb * w)
    pos1 = lax.broadcasted_iota(jnp.int32, (r, gb * w), 0) % SUBLANES
    x_re = _seg_shift(l_re, 1, pos1, SUBLANES, fwd(r)) + p_re * z_re - p_im * z_im
    x_im = _seg_shift(l_im, 1, pos1, SUBLANES, fwd(r)) + p_re * z_im + p_im * z_re

    for g in range(gb):
        sl = slice(g * w, (g + 1) * w)
        xin = jnp.concatenate([x_re[:, sl], x_im[:, sl]], axis=1).astype(BF16)
        t_op = jnp.concatenate(
            [t_ref[g, (S5_CHUNK - 1 - t) * SSM_GROUP:(2 * S5_CHUNK - 1 - t) * SSM_GROUP, :]
             for t in range(S5_CHUNK)], axis=1).astype(BF16)
        y = (jnp.dot(u_ref[g], t_op, preferred_element_type=F32)
             + jnp.dot(xin, wo_ref[g], preferred_element_type=F32))
        z_ref[g] = jax.nn.gelu(y).astype(z_ref.dtype)


def _s5_scan(u5, taps, ws, wo, dec, pows, expand, *, nk, gb):
    g, r, cw = u5.shape
    w = 2 * SSM_STATE
    grp = lambda i: (i, 0, 0)
    par = lambda a: pl.BlockSpec((None,) + a.shape[1:], lambda i: (i, 0, 0, 0))
    return pl.pallas_call(
        functools.partial(_s5_kernel, gb=gb, nk=nk),
        grid=(g // gb,),
        in_specs=[pl.BlockSpec((gb, r, cw), grp),
                  pl.BlockSpec((gb,) + taps.shape[1:], grp), pl.BlockSpec((gb, cw, 2 * w), grp),
                  pl.BlockSpec((gb, 2 * w, cw), grp),
                  par(dec), par(pows), _const_spec(expand.shape)],
        out_specs=pl.BlockSpec((gb, r, cw), grp),
        out_shape=jax.ShapeDtypeStruct((g, r, cw), BF16),
        scratch_shapes=[pltpu.VMEM((2, gb, r, w), F32)],
        compiler_params=pltpu.CompilerParams(
            dimension_semantics=("arbitrary",), vmem_limit_bytes=VMEM_LIMIT_BYTES),
        name="s5_scan",
    )(u5, taps, ws, wo, dec, pows, expand)


def _s5_operators(fwd, bwd, d_skip, n_doublings):
    L = S5_CHUNK
    hi = lax.Precision.HIGHEST

    def disc(lam_re, lam_im, log_dt, b_re, b_im, c_re, c_im):
        lr, li = lam_re.astype(F32), lam_im.astype(F32)
        dt = jnp.exp(log_dt.astype(F32))[:, None]
        steps = jnp.arange(L + 1, dtype=F32)
        mag = jnp.exp((lr * dt)[..., None] * steps)
        ang = (li * dt)[..., None] * steps
        pw_re, pw_im = mag * jnp.cos(ang), mag * jnp.sin(ang)
        xr, xi = pw_re[..., 1] - 1.0, pw_im[..., 1]
        den = lr * lr + li * li
        kr, ki = (xr * lr + xi * li) / den, (xi * lr - xr * li) / den
        br, bi = b_re.astype(F32), b_im.astype(F32)
        bb_re = kr[..., None] * br - ki[..., None] * bi
        bb_im = kr[..., None] * bi + ki[..., None] * br
        e_re = pw_re[..., :L, None] * bb_re[:, :, None, :] - pw_im[..., :L, None] * bb_im[:, :, None, :]
        e_im = pw_re[..., :L, None] * bb_im[:, :, None, :] + pw_im[..., :L, None] * bb_re[:, :, None, :]
        cr, ci = c_re.astype(F32), c_im.astype(F32)
        taps = (jnp.einsum('gcp,gptd->gtdc', cr, e_re, precision=hi)
                - jnp.einsum('gcp,gptd->gtdc', ci, e_im, precision=hi))
        return pw_re, pw_im, e_re, e_im, cr, ci, taps

    pwr_f, pwi_f, er_f, ei_f, cr_f, ci_f, taps_f = disc(*fwd)
    pwr_b, pwi_b, er_b, ei_b, cr_b, ci_b, taps_b = disc(*bwd)
    G, P = pwr_f.shape[:2]
    C = er_f.shape[-1]
    lag0 = taps_f[:, 0] + taps_b[:, 0] + jnp.eye(C, dtype=F32)[None] * d_skip.astype(F32)[:, None, :]
    taps = jnp.concatenate([taps_f[:, :0:-1], lag0[:, None], taps_b[:, 1:]], axis=1)
    taps = taps.reshape(G, (2 * L - 1) * C, C)

    to_sdp = lambda e: e.transpose(0, 2, 3, 1)
    ws = jnp.concatenate(
        [to_sdp(er_f[:, :, ::-1]), to_sdp(er_b), to_sdp(ei_f[:, :, ::-1]), to_sdp(ei_b)],
        axis=-1).reshape(G, L * C, 4 * P)

    def out_map(cr, ci, pr, pi):
        c_r, c_i = cr.transpose(0, 2, 1)[:, :, None, :], ci.transpose(0, 2, 1)[:, :, None, :]
        p_r, p_i = pr[..., None], pi[..., None]
        return ((c_r * p_r - c_i * p_i).reshape(G, P, L * C),
                (c_r * p_i + c_i * p_r).reshape(G, P, L * C))

    of_re, of_im = out_map(cr_f, ci_f, pwr_f[..., 1:L + 1], pwi_f[..., 1:L + 1])
    ob_re, ob_im = out_map(cr_b, ci_b, pwr_b[..., L:0:-1], pwi_b[..., L:0:-1])
    wo = jnp.concatenate([of_re, ob_re, -of_im, -ob_im], axis=1)

    def chunk_pow(lam_re, lam_im, log_dt, mult):
        dt = jnp.exp(log_dt.astype(F32))[:, None]
        mag = jnp.exp((lam_re.astype(F32) * dt)[:, None, :] * (L * mult)[None, :, None])
        ang = (lam_im.astype(F32) * dt)[:, None, :] * (L * mult)[None, :, None]
        return mag * jnp.cos(ang), mag * jnp.sin(ang)

    doubling = 2.0 ** jnp.arange(n_doublings, dtype=F32)
    in_tile = jnp.arange(SUBLANES, dtype=F32)
    dec = [jnp.concatenate([f, b], axis=-1) for f, b in
           zip(chunk_pow(*fwd[:3], doubling), chunk_pow(*bwd[:3], doubling))]
    pows = [jnp.concatenate([f, b], axis=-1) for f, b in
            zip(chunk_pow(*fwd[:3], in_tile), chunk_pow(*bwd[:3], in_tile[::-1]))]
    return (taps, ws.astype(BF16), wo.astype(BF16),
            jnp.stack(dec, axis=1), jnp.stack(pows, axis=1))


def _merge_ln_kernel(x_ref, o_ref, z_ref, sa_ref, sb_ref, woa_ref, wglu_ref, wos_ref, wout_ref,
                     g_ref, b_ref, out_ref, z_sc):
    y_a = jnp.dot(o_ref[...], woa_ref[...], preferred_element_type=F32)
    z = _from_chunk_layout(z_ref, z_sc)
    gl = jax.nn.sigmoid(jnp.dot(z.astype(BF16), wglu_ref[...], preferred_element_type=F32))
    y_b = jnp.dot((z * gl).astype(BF16), wos_ref[...], preferred_element_type=F32)
    merged = (sa_ref[...].astype(F32) * y_a + sb_ref[...].astype(F32) * y_b).astype(BF16)
    r = ALPHA * x_ref[...] + jnp.dot(merged, wout_ref[...], preferred_element_type=F32)
    out_ref[...] = _layer_norm(r, g_ref[...], b_ref[...])


def _merge_ln(x1, o, z, sa, sb, woa, wglu, wos, wout, g, b, *, seq, tm):
    n, d = x1.shape
    groups, _, chunk_w = z.shape
    spb = seq // tm
    return pl.pallas_call(
        _merge_ln_kernel,
        grid=(n // tm,),
        in_specs=[pl.BlockSpec((tm, d), lambda i: (i, 0)),
                  pl.BlockSpec((tm, o.shape[1]), lambda i: (i, 0)),
                  pl.BlockSpec((groups, tm // S5_CHUNK, chunk_w), lambda i: (0, i, 0)),
                  pl.BlockSpec((tm, d), lambda i: (i, 0)),
                  pl.BlockSpec((tm, d), lambda i: (i, 0)),
                  _const_spec(woa.shape), _const_spec(wglu.shape), _const_spec(wos.shape),
                  _const_spec(wout.shape), _const_spec((1, d)), _const_spec((1, d))],
        out_specs=pl.BlockSpec((tm, d), lambda i: (i, 0)),
        out_shape=jax.ShapeDtypeStruct((n, d), F32),
        scratch_shapes=[pltpu.VMEM((groups * SSM_GROUP // LANES, tm, LANES), F32)],
        compiler_params=pltpu.CompilerParams(
            dimension_semantics=("arbitrary",), vmem_limit_bytes=VMEM_LIMIT_BYTES),
        name="merge_ln",
    )(x1, o, z, sa, sb, woa, wglu, wos, wout, g, b)


def _rot_cols(w):
    half = w.shape[-1] // 2
    return jnp.concatenate([-w[..., half:], w[..., :half]], axis=-1)


def _pad_heads(w):
    r, h, dh = w.shape
    return jnp.pad(w, ((0, 0), (0, 0), (0, HEAD_PAD - dh))).reshape(r, h * HEAD_PAD)


def _pad_heads_paired(w):
    r, h, dh = w.shape
    halves = HEAD_PAD // dh
    pick = jnp.eye(halves, dtype=w.dtype)[None, None, :, :, None]
    return (w.reshape(r, h // halves, halves, 1, dh) * pick).reshape(r, h * HEAD_PAD)


def _rope_tables(seq, scale):
    lane = jnp.arange(LANES)
    half = QK_ROPE // 2
    pos = jnp.arange(seq, dtype=F32)[:, None]

    def angle(first_lane):
        idx = ((lane - first_lane) % half).astype(F32)
        return pos * (ROPE_THETA ** (-(2.0 * idx) / QK_ROPE))[None, :]

    in_rope = (lane >= QK_NOPE) & (lane < QK_NOPE + QK_ROPE)
    aq = angle(QK_NOPE)
    cosq = jnp.where(lane < QK_NOPE, scale, jnp.where(in_rope, jnp.cos(aq) * scale, 0.0))
    sinq = jnp.where(in_rope, jnp.sin(aq) * scale, 0.0)
    ak = angle(0)
    tkr = jnp.where(lane < QK_ROPE, jnp.cos(ak), jnp.where(lane < 2 * QK_ROPE, jnp.sin(ak), 0.0))
    return cosq, sinq, tkr


def kernel(x, p, ffn1_w1, ffn1_w3, ffn1_w2, ln1_g, ln1_b, w_in, q_norm_g, kv_norm_g, w_uq, w_uk, w_uv, w_o_attn, ssm_lam_re_f, ssm_lam_im_f, ssm_log_dt_f, ssm_b_re_f, ssm_b_im_f, ssm_c_re_f, ssm_c_im_f, ssm_lam_re_b, ssm_lam_im_b, ssm_log_dt_b, ssm_b_re_b, ssm_b_im_b, ssm_c_re_b, ssm_c_im_b, ssm_d, w_glu, w_o_ssm, w_out, ln2_g, ln2_b, ffn2_w1, ffn2_w3, ffn2_w2, ln3_g, ln3_b, ple_w_proj, ple_w_gate, ln4_g, ln4_b):
    batch, seq, d = x.shape
    n = batch * seq
    ssm_w = w_glu.shape[1]
    groups = ssm_w // SSM_GROUP
    assert DEPTH == 1 and ffn1_w1.shape[0] == 1
    assert seq % (SUBLANES * S5_CHUNK) == 0 and groups % S5_GROUP_BLOCK == 0
    assert ssm_w % LANES == 0
    tm = min(512, seq)
    tq = min(1024, seq)
    i = 0
    row = lambda a: a[i].reshape(1, -1).astype(F32)

    x0 = x.reshape(n, d)
    x1 = _ffn_ln(x0, ffn1_w1[i].astype(BF16), ffn1_w3[i].astype(BF16), ffn1_w2[i].astype(BF16),
                 row(ln1_g), row(ln1_b), tm=tm)

    o_kv = Q_LORA
    o_kr = o_kv + KV_LORA
    o_u = o_kr + QK_ROPE
    o_ga = o_u + ssm_w
    o_gb = o_ga + d
    w = w_in[i]
    w_kr = w[:, o_kr:o_u]
    kr_block = jnp.concatenate(
        [w_kr, _rot_cols(w_kr), jnp.zeros((d, LANES - 2 * QK_ROPE), w.dtype)], axis=1)
    win_ext = jnp.concatenate(
        [w[:, :o_kr], w[:, o_u:o_ga], w[:, o_ga:o_gb], w[:, o_gb:], kr_block], axis=1).astype(BF16)
    wq = w_uq[i]
    wq_rot = jnp.concatenate(
        [jnp.zeros_like(wq[..., :QK_NOPE]), _rot_cols(wq[..., QK_NOPE:])], axis=-1)
    wq_p = _pad_heads(wq).astype(BF16)
    wqr_p = _pad_heads(wq_rot).astype(BF16)
    wk_p = _pad_heads(w_uk[i]).astype(BF16)
    wv_pt = _pad_heads_paired(w_uv[i]).T.astype(BF16)
    src = jnp.arange(LANES)[:, None]
    dst = jnp.arange(MLA_HEADS * HEAD_PAD)[None, :] % HEAD_PAD - QK_NOPE
    e2 = ((src < 2 * QK_ROPE) & (src % QK_ROPE == dst)).astype(BF16)
    cosq, sinq, tkr = _rope_tables(seq, (QK_NOPE + QK_ROPE) ** -0.5 * math.log2(math.e))

    q, k, vt, u5, sa, sb = _proj_mla(
        x1, win_ext, row(q_norm_g), row(kv_norm_g), wq_p, wqr_p, wk_p, wv_pt, e2, cosq, sinq, tkr,
        batch=batch, seq=seq, tm=tm, ssm_w=ssm_w)

    o = _flash_attn(q, k, vt, tq=tq).reshape(n, -1)

    nk = seq // S5_CHUNK
    fwd = (ssm_lam_re_f[i], ssm_lam_im_f[i], ssm_log_dt_f[i], ssm_b_re_f[i], ssm_b_im_f[i],
           ssm_c_re_f[i], ssm_c_im_f[i])
    bwd = (ssm_lam_re_b[i], ssm_lam_im_b[i], ssm_log_dt_b[i], ssm_b_re_b[i], ssm_b_im_b[i],
           ssm_c_re_b[i], ssm_c_im_b[i])
    tiles_per_seq = nk // SUBLANES
    n_doublings = 3 + max(0, (tiles_per_seq - 1).bit_length())
    taps, ws, wo, dec, pows = _s5_operators(
        fwd, bwd, ssm_d[i].reshape(groups, SSM_GROUP), n_doublings)
    gb = S5_GROUP_BLOCK
    per_block = lambda a: (a.reshape(groups // gb, gb, *a.shape[1:]).transpose(0, 2, 3, 1, 4)
                           .reshape(groups // gb, a.shape[1], a.shape[2], gb * a.shape[3]))
    rows = n // S5_CHUNK
    expand = (jnp.arange(rows)[:, None] // SUBLANES == jnp.arange(rows // SUBLANES)[None, :]).astype(BF16)
    z5 = _s5_scan(u5, taps, ws, wo, per_block(dec), per_block(pows), expand, nk=nk, gb=gb)

    x2 = _merge_ln(x1, o, z5, sa, sb, w_o_attn[i].astype(BF16), w_glu[i].astype(BF16),
                   w_o_ssm[i].astype(BF16), w_out[i].astype(BF16), row(ln2_g), row(ln2_b),
                   seq=seq, tm=tm)

    x4 = _ffn_ple_ln(x2, p[i].reshape(n, -1), ffn2_w1[i].astype(BF16), ffn2_w3[i].astype(BF16),
                     ffn2_w2[i].astype(BF16), row(ln3_g), row(ln3_b),
                     ple_w_gate[i].astype(BF16), ple_w_proj[i].astype(BF16),
                     row(ln4_g), row(ln4_b), tm=tm)
    return x4.reshape(batch, seq, d)
```
